```python
import jax, jax.numpy as jnp
from jax import lax
import numpy as np

D_MODEL = 1024
BATCH = 4
SEQ = 8192
DEPTH = 1

SSM_HEADDIM = 64
SSM_HEADS = 16
SSM_INNER = SSM_HEADS * SSM_HEADDIM
SSM_GROUPS = 2
SSM_HPG = SSM_HEADS // SSM_GROUPS
SSM_STATE = 64
CONV_K = 4
CHUNK = 128
CONV_DIM = SSM_INNER + 2 * SSM_GROUPS * SSM_STATE

ATT_HEADS = 8
ATT_HEADDIM = 128
ATT_INNER = ATT_HEADS * ATT_HEADDIM
Q_BLOCK = 128

N_BRANCHES = 2
EPS = 1e-6

IN_SPLITS = (SSM_INNER, CONV_DIM, SSM_HEADS, ATT_INNER, ATT_INNER, ATT_INNER, ATT_INNER, ATT_HEADS, N_BRANCHES * D_MODEL)
IN_COLS = int(sum(IN_SPLITS))

kernel_name = "hybrid_ssd_fox_gated_block"


def _rmsnorm(x, w):
    xf = x.astype(jnp.float32)
    y = xf * lax.rsqrt(jnp.mean(xf * xf, axis=-1, keepdims=True) + EPS)
    return (y * w.astype(jnp.float32)).astype(x.dtype)


def _causal_depthwise_conv(u, w, b):
    ch = u.shape[-1]
    out = lax.conv_general_dilated(
        u, w[:, None, :].astype(u.dtype), window_strides=(1,), padding=[(CONV_K - 1, 0)],
        dimension_numbers=("NWC", "WIO", "NWC"), feature_group_count=ch)
    return out + b.astype(u.dtype)


def _ssd_branch(z, xbc, dt_raw, conv_w, conv_b, dt_bias, a_log, d_skip, ssm_norm_w):
    bsz, s, _ = xbc.shape
    g, r, p, n = SSM_GROUPS, SSM_HPG, SSM_HEADDIM, SSM_STATE
    nc = s // CHUNK
    xbc = jax.nn.silu(_causal_depthwise_conv(xbc, conv_w, conv_b))
    xs, bm, cm = jnp.split(xbc, [SSM_INNER, SSM_INNER + g * n], axis=-1)
    xs = xs.reshape(bsz, s, g, r, p)
    dt = jax.nn.softplus(dt_raw.astype(jnp.float32) + dt_bias.astype(jnp.float32)).reshape(bsz, s, g, r)
    a = dt * (-jnp.exp(a_log.astype(jnp.float32))).reshape(g, r)
    xdt = xs.astype(jnp.float32) * dt[..., None]
    xc = xdt.reshape(bsz, nc, CHUNK, g, r, p)
    bc = bm.astype(jnp.float32).reshape(bsz, nc, CHUNK, g, n)
    cc = cm.astype(jnp.float32).reshape(bsz, nc, CHUNK, g, n)
    ac = a.reshape(bsz, nc, CHUNK, g, r).transpose(0, 3, 4, 1, 2)
    a_cs = jnp.cumsum(ac, axis=-1)
    tril = jnp.tril(jnp.ones((CHUNK, CHUNK), dtype=bool))
    seg = a_cs[..., :, None] - a_cs[..., None, :]
    lmat = jnp.exp(jnp.where(tril, seg, -jnp.inf))
    cb = jnp.einsum("bclgn,bcsgn->bgcls", cc, bc)
    y_diag = jnp.einsum("bgrcls,bcsgrp->bclgrp", cb[:, :, None] * lmat, xc)
    decay_states = jnp.exp(a_cs[..., -1:] - a_cs)
    states = jnp.einsum("bclgn,bgrcl,bclgrp->bcgrpn", bc, decay_states, xc)
    states = jnp.concatenate([jnp.zeros_like(states[:, :1]), states], axis=1)
    cs = jnp.cumsum(jnp.pad(a_cs[..., -1], ((0, 0), (0, 0), (0, 0), (1, 0))), axis=-1)
    tril_c = jnp.tril(jnp.ones((nc + 1, nc + 1), dtype=bool))
    decay_chunk = jnp.exp(jnp.where(tril_c, cs[..., :, None] - cs[..., None, :], -jnp.inf))
    new_states = jnp.einsum("bgrzc,bcgrpn->bzgrpn", decay_chunk, states)
    prev_states = new_states[:, :-1]
    y_off = jnp.einsum("bclgn,bcgrpn,bgrcl->bclgrp", cc, prev_states, jnp.exp(a_cs))
    y = (y_diag + y_off).reshape(bsz, s, g, r, p) + d_skip.astype(jnp.float32).reshape(g, r)[..., None] * xs.astype(jnp.float32)
    y = y.reshape(bsz, s, SSM_INNER).astype(z.dtype)
    return _rmsnorm(y * jax.nn.silu(z), ssm_norm_w)


def _fox_branch(q, k, v, z, f_logit, b_f):
    bsz, s, _ = q.shape
    nb = s // Q_BLOCK
    q = q.reshape(bsz, s, ATT_HEADS, ATT_HEADDIM).transpose(0, 2, 1, 3)
    k = k.reshape(bsz, s, ATT_HEADS, ATT_HEADDIM).transpose(0, 2, 1, 3)
    v = v.reshape(bsz, s, ATT_HEADS, ATT_HEADDIM).transpose(0, 2, 1, 3)
    log_f = jax.nn.log_sigmoid(f_logit.astype(jnp.float32) + b_f.astype(jnp.float32))
    f_cum = jnp.cumsum(log_f, axis=1).transpose(0, 2, 1)
    scale = ATT_HEADDIM ** -0.5
    qb = q.reshape(bsz, ATT_HEADS, nb, Q_BLOCK, ATT_HEADDIM).transpose(2, 0, 1, 3, 4)
    fq = f_cum.reshape(bsz, ATT_HEADS, nb, Q_BLOCK).transpose(2, 0, 1, 3)
    qpos = jnp.arange(s, dtype=jnp.int32).reshape(nb, Q_BLOCK)
    kpos = jnp.arange(s, dtype=jnp.int32)

    def one_block(args):
        qi, fi, pi = args
        logits = jnp.einsum("bhqd,bhkd->bhqk", qi, k).astype(jnp.float32) * scale
        logits = logits + fi[..., :, None] - f_cum[..., None, :]
        logits = jnp.where(pi[:, None] >= kpos[None, :], logits, -jnp.inf)
        probs = jax.nn.softmax(logits, axis=-1).astype(v.dtype)
        return jnp.einsum("bhqk,bhkd->bhqd", probs, v)

    o = lax.map(one_block, (qb, fq, qpos))
    o = o.transpose(1, 0, 3, 2, 4).reshape(bsz, s, ATT_INNER)
    return o * jax.nn.silu(z)


def setup_inputs(seed: int = 0) -> dict:
    key = jax.random.key(seed)
    ks = jax.random.split(key, 20)
    d = D_MODEL
    x = jax.random.normal(ks[0], (BATCH, SEQ, d), jnp.float32)
    c = jax.random.normal(ks[1], (BATCH, d), jnp.float32)
    w_ada = jax.random.normal(ks[2], (d, 3 * d), jnp.float32) * d ** -0.5 * 0.5
    b_ada = jax.random.normal(ks[3], (3 * d,), jnp.float32) * 0.01
    norm_w = 1.0 + 0.05 * jax.random.normal(ks[4], (d,), jnp.float32)
    w_in = jax.random.normal(ks[5], (d, IN_COLS), jnp.float32) * d ** -0.5
    conv_w = jax.random.normal(ks[6], (CONV_K, CONV_DIM), jnp.float32) * 0.5
    conv_b = jax.random.normal(ks[7], (CONV_DIM,), jnp.float32) * 0.02
    dt0 = jnp.exp(jax.random.uniform(ks[8], (SSM_HEADS,), jnp.float32, np.log(1e-3), np.log(1e-1)))
    dt_bias = dt0 + jnp.log(-jnp.expm1(-dt0))
    a_log = jnp.log(jax.random.uniform(ks[9], (SSM_HEADS,), jnp.float32, 1.0, 16.0))
    d_skip = 1.0 + 0.1 * jax.random.normal(ks[10], (SSM_HEADS,), jnp.float32)
    ssm_norm_w = 1.0 + 0.05 * jax.random.normal(ks[11], (SSM_INNER,), jnp.float32)
    b_f = jnp.linspace(1.0, 6.0, ATT_HEADS, dtype=jnp.float32) + 0.1 * jax.random.normal(ks[12], (ATT_HEADS,), jnp.float32)
    b_gate = jax.random.normal(ks[13], (N_BRANCHES * d,), jnp.float32) * 0.01
    w_proj_ssm = jax.random.normal(ks[14], (SSM_INNER, d), jnp.float32) * SSM_INNER ** -0.5
    w_proj_att = jax.random.normal(ks[15], (ATT_INNER, d), jnp.float32) * ATT_INNER ** -0.5
    w_out = jax.random.normal(ks[16], (d, d), jnp.float32) * d ** -0.5
    final_norm_w = 1.0 + 0.05 * jax.random.normal(ks[17], (d,), jnp.float32)
    return {"x": x, "c": c, "w_ada": w_ada, "b_ada": b_ada, "norm_w": norm_w, "w_in": w_in,
            "conv_w": conv_w, "conv_b": conv_b, "dt_bias": dt_bias, "a_log": a_log, "d_skip": d_skip,
            "ssm_norm_w": ssm_norm_w, "b_f": b_f, "b_gate": b_gate, "w_proj_ssm": w_proj_ssm,
            "w_proj_att": w_proj_att, "w_out": w_out, "final_norm_w": final_norm_w}


def reference(x, c, w_ada, b_ada, norm_w, w_in, conv_w, conv_b, dt_bias, a_log, d_skip,
              ssm_norm_w, b_f, b_gate, w_proj_ssm, w_proj_att, w_out, final_norm_w):
    ada = jax.nn.silu(c) @ w_ada + b_ada
    shift, scale, gate = jnp.split(ada, 3, axis=-1)
    offsets = [int(o) for o in np.cumsum(IN_SPLITS)[:-1]]
    for _ in range(DEPTH):
        h = _rmsnorm(x, norm_w) * (1.0 + scale[:, None, :]) + shift[:, None, :]
        proj = jnp.einsum("bsd,de->bse", h, w_in)
        z_ssm, xbc, dt_raw, q, k, v, z_att, f_logit, g_logit = jnp.split(proj, offsets, axis=-1)
        y_ssm = _ssd_branch(z_ssm, xbc, dt_raw, conv_w, conv_b, dt_bias, a_log, d_skip, ssm_norm_w)
        y_att = _fox_branch(q, k, v, z_att, f_logit, b_f)
        g_ssm, g_att = jnp.split(jax.nn.sigmoid(g_logit + b_gate), N_BRANCHES, axis=-1)
        merged = g_ssm * jnp.einsum("bse,ed->bsd", y_ssm, w_proj_ssm) + g_att * jnp.einsum("bse,ed->bsd", y_att, w_proj_att)
        x = x + gate[:, None, :] * jnp.einsum("bsd,de->bse", merged, w_out)
    return _rmsnorm(x, final_norm_w)
```

```python
import functools

import jax
import jax.numpy as jnp
import numpy as np
from jax import lax
from jax.experimental import pallas as pl
from jax.experimental.pallas import tpu as pltpu

F32 = jnp.float32
BF16 = jnp.bfloat16

D_MODEL = 1024
SSM_HEADDIM = 64
SSM_HEADS = 16
SSM_INNER = SSM_HEADS * SSM_HEADDIM
SSM_GROUPS = 2
SSM_STATE = 64
CONV_K = 4
CHUNK = 128
ATT_HEADS = 8
ATT_HEADDIM = 128
ATT_INNER = ATT_HEADS * ATT_HEADDIM
EPS = 1e-6

LANES = 128
GROUP_W = SSM_INNER // SSM_GROUPS
BC_W = 2 * SSM_GROUPS * SSM_STATE

COL_Z_SSM = 0
COL_XS = 1024
COL_Q = 2048
COL_K = 3072
COL_V = 4096
COL_Z_ATT = 5120
COL_G = 6144
COL_BC = 8192
P_COLS = 8448

DT_LANE0 = 0
F_LANE0 = SSM_HEADS
ONE_LANE = LANES - 1

VMEM_LIMIT = 56 * 1024 * 1024


def _dot(a, b):
    return jnp.dot(a, b, preferred_element_type=F32)


def _split2(x):
    hi = x.astype(BF16)
    lo = (x - hi.astype(F32)).astype(BF16)
    return hi, lo


def _split3(x):
    hi = x.astype(BF16)
    r1 = x - hi.astype(F32)
    mid = r1.astype(BF16)
    lo = (r1 - mid.astype(F32)).astype(BF16)
    return hi, mid, lo


def _ada_kernel(c_ref, w_ref, b_ref, o_ref):
    c = c_ref[...]
    cs = c * jax.nn.sigmoid(c)
    w = w_ref[...]
    c_hi, c_lo = _split2(cs)
    w_hi, w_lo = _split2(w)
    o_ref[...] = _dot(c_hi, w_hi) + _dot(c_lo, w_hi) + _dot(c_hi, w_lo) + b_ref[...]


def _ada(c_pad, w_ada, b_ada):
    rows = c_pad.shape[0]
    n = w_ada.shape[1]
    tn = 1024
    return pl.pallas_call(
        _ada_kernel,
        grid=(n // tn,),
        in_specs=[
            pl.BlockSpec((rows, D_MODEL), lambda j: (0, 0)),
            pl.BlockSpec((D_MODEL, tn), lambda j: (0, j)),
            pl.BlockSpec((1, tn), lambda j: (0, j)),
        ],
        out_specs=pl.BlockSpec((rows, tn), lambda j: (0, j)),
        out_shape=jax.ShapeDtypeStruct((rows, n), F32),
        compiler_params=pltpu.CompilerParams(
            dimension_semantics=("arbitrary",), vmem_limit_bytes=VMEM_LIMIT),
        name="ada",
    )(c_pad, w_ada, b_ada)


INPROJ_TM = 1024
INPROJ_TN = 2816
INPROJ_CW = 256


def _inproj_kernel(x_ref, nw_ref, scale_ref, shift_ref, w_ref, ws_ref, cs_ref,
                   p_ref, sm_ref, h_ref):
    @pl.when(pl.program_id(2) == 0)
    def _():
        x = x_ref[0]
        ms = jnp.mean(x * x, axis=-1, keepdims=True)
        y = x * lax.rsqrt(ms + EPS) * nw_ref[...]
        h = y * (1.0 + scale_ref[0]) + shift_ref[0]
        h_hi, h_lo = _split2(h)
        h_ref[...] = h_hi
        sm_ref[0] = _dot(h_hi, ws_ref[0]) + _dot(h_lo, ws_ref[0]) + _dot(h_hi, ws_ref[1])

    h = h_ref[...]
    for c in range(INPROJ_TN // INPROJ_CW):
        sl = slice(c * INPROJ_CW, (c + 1) * INPROJ_CW)
        acc = _dot(h, w_ref[:, sl])
        p_ref[0, :, sl] = (acc * cs_ref[:, sl]).astype(BF16)


def _inproj(x, norm_w, scale, shift, w_big, w_small, colscale):
    b, s, d = x.shape
    tm, tn = INPROJ_TM, INPROJ_TN
    return pl.pallas_call(
        _inproj_kernel,
        grid=(b, s // tm, P_COLS // tn),
        in_specs=[
            pl.BlockSpec((1, tm, d), lambda bi, i, j: (bi, i, 0)),
            pl.BlockSpec((1, d), lambda bi, i, j: (0, 0)),
            pl.BlockSpec((1, 1, d), lambda bi, i, j: (bi, 0, 0)),
            pl.BlockSpec((1, 1, d), lambda bi, i, j: (bi, 0, 0)),
            pl.BlockSpec((d, tn), lambda bi, i, j: (0, j)),
            pl.BlockSpec((2, d, LANES), lambda bi, i, j: (0, 0, 0)),
            pl.BlockSpec((1, tn), lambda bi, i, j: (0, j)),
        ],
        out_specs=[
            pl.BlockSpec((1, tm, tn), lambda bi, i, j: (bi, i, j)),
            pl.BlockSpec((1, tm, LANES), lambda bi, i, j: (bi, i, 0)),
        ],
        out_shape=[
            jax.ShapeDtypeStruct((b, s, P_COLS), BF16),
            jax.ShapeDtypeStruct((b, s, LANES), F32),
        ],
        scratch_shapes=[pltpu.VMEM((tm, d), BF16)],
        compiler_params=pltpu.CompilerParams(
            dimension_semantics=("arbitrary", "arbitrary", "arbitrary"),
            vmem_limit_bytes=VMEM_LIMIT),
        name="inproj",
    )(x, norm_w, scale, shift, w_big, w_small, colscale)


SSD_ROWS = 512
TAIL = 8


def _ssd_kernel(z_ref, xs_ref, bc_ref, sm_ref, cwx_ref, cbx_ref, cwbc_ref, cbbc_ref,
                bias_ref, alog_ref, dskip_ref, nw_ref, tril_ref, ee_ref, eq_ref, ek_ref,
                y_ref, qa_ref, ka_ref,
                extx_ref, extbc_ref, xc_ref, bcc_ref, state_ref, carry_ref):
    rows = SSD_ROWS
    L = CHUNK

    @pl.when(pl.program_id(1) == 0)
    def _():
        extx_ref[0:TAIL, :] = jnp.zeros((TAIL, SSM_INNER), F32)
        extbc_ref[0:TAIL, :] = jnp.zeros((TAIL, BC_W), F32)
        state_ref[...] = jnp.zeros(state_ref.shape, F32)
        carry_ref[...] = jnp.zeros(carry_ref.shape, F32)

    extx_ref[TAIL:TAIL + rows, :] = xs_ref[0].astype(F32)
    extbc_ref[TAIL:TAIL + rows, :] = bc_ref[0].astype(F32)

    def conv_silu(ext_ref, w_ref, b_ref):
        acc = b_ref[...] + w_ref[CONV_K - 1:CONV_K, :] * ext_ref[TAIL:TAIL + rows, :]
        for k in range(CONV_K - 1):
            off = TAIL - (CONV_K - 1) + k
            acc = acc + w_ref[k:k + 1, :] * ext_ref[off:off + rows, :]
        return acc * jax.nn.sigmoid(acc)

    xc_ref[...] = conv_silu(extx_ref, cwx_ref, cbx_ref)
    bcc_ref[...] = conv_silu(extbc_ref, cwbc_ref, cbbc_ref)
    extx_ref[0:TAIL, :] = extx_ref[rows:rows + TAIL, :]
    extbc_ref[0:TAIL, :] = extbc_ref[rows:rows + TAIL, :]

    lane = lax.broadcasted_iota(jnp.int32, (L, LANES), 1)
    is_dt = lane < SSM_HEADS
    is_f = jnp.logical_and(lane >= F_LANE0, lane < F_LANE0 + ATT_HEADS)
    row_i = lax.broadcasted_iota(jnp.int32, (L, L), 0)
    col_i = lax.broadcasted_iota(jnp.int32, (L, L), 1)
    causal = row_i >= col_i
    neg_a = -jnp.exp(alog_ref[...])
    tril = tril_ref[...]

    def chunk(c, carry_unused):
        r0 = pl.multiple_of(c * L, L)
        v = sm_ref[0, pl.ds(r0, L), :] + bias_ref[...]
        t = jnp.log1p(jnp.exp(-jnp.abs(v)))
        dt = jnp.maximum(v, 0.0) + t
        lf = jnp.minimum(v, 0.0) - t
        val = jnp.where(is_dt, dt * neg_a, jnp.where(is_f, lf, 0.0))
        v_hi, v_mid, v_lo = _split3(val)
        cs3 = _dot(tril, jnp.concatenate([v_hi, v_mid, v_lo], axis=1))
        cs = cs3[:, 0:LANES] + cs3[:, LANES:2 * LANES] + cs3[:, 2 * LANES:3 * LANES]

        fc = jnp.where(is_f, cs + carry_ref[...], 0.0)
        carry_ref[...] = fc[L - 1:L, :]
        fa = jnp.where(lane == ONE_LANE, 1.0, fc)
        f_hi, f_mid, f_lo = _split3(fa)
        f3 = jnp.concatenate([f_hi, f_mid, f_lo], axis=1)
        qa_ref[0, pl.ds(r0, L), :] = _dot(f3, eq_ref[...]).astype(BF16)
        ka_ref[0, pl.ds(r0, L), :] = _dot(f3, ek_ref[...]).astype(BF16)

        a_cs = jnp.where(is_dt, cs, 0.0)
        a_last = a_cs[L - 1:L, :]
        dtm = jnp.where(is_dt, dt, 0.0)
        w2 = jnp.where(is_dt, jnp.exp(a_cs), 0.0)
        w1 = dtm * jnp.exp(a_last - a_cs)
        st_hi, st_lo = _split2(jnp.concatenate([dtm, w2, w1], axis=0))
        ex = _dot(jnp.concatenate([st_hi, st_lo], axis=1), ee_ref[...])
        dt_e = ex[0:L]
        w2_e = ex[L:2 * L]
        w1_e = ex[2 * L:3 * L]

        xsc = xc_ref[pl.ds(r0, L), :]
        bcv = bcc_ref[pl.ds(r0, L), :]
        b_all = bcv[:, 0:LANES]
        c_all = bcv[:, LANES:2 * LANES]
        b_t = b_all.T.astype(BF16)
        xd = xsc * dt_e
        xw1 = (xsc * w1_e).astype(BF16)
        a_cs_t = a_cs.T

        ys = []
        for g in range(SSM_GROUPS):
            gsl = slice(g * GROUP_W, (g + 1) * GROUP_W)
            in_g = jnp.logical_and(lane >= g * SSM_STATE, lane < (g + 1) * SSM_STATE)
            c_m = jnp.where(in_g, c_all, 0.0).astype(BF16)
            cb = _dot(c_m, b_t)
            s_g = state_ref[g]
            y_off = _dot(c_m, s_g.astype(BF16)) * w2_e[:, gsl]
            pieces = []
            for j in range(GROUP_W // LANES):
                h_a = g * (SSM_HEADS // SSM_GROUPS) + 2 * j
                xp = xd[:, g * GROUP_W + j * LANES:g * GROUP_W + (j + 1) * LANES]
                x_a = jnp.where(lane < SSM_HEADDIM, xp, 0.0).astype(BF16)
                x_b = jnp.where(lane >= SSM_HEADDIM, xp, 0.0).astype(BF16)

                def lmat(h):
                    seg = a_cs[:, h:h + 1] - a_cs_t[h:h + 1, :]
                    return (cb * jnp.exp(jnp.where(causal, seg, -jnp.inf))).astype(BF16)

                pieces.append(_dot(lmat(h_a), x_a) + _dot(lmat(h_a + 1), x_b))
            ys.append(jnp.concatenate(pieces, axis=1) + y_off)
            state_ref[g] = s_g * w2_e[L - 1:L, gsl] + _dot(b_t, xw1[:, gsl])

        y = jnp.concatenate(ys, axis=1) + dskip_ref[...] * xsc
        z = z_ref[0, pl.ds(r0, L), :].astype(F32)
        yg = y * (z * jax.nn.sigmoid(z))
        ms = jnp.mean(yg * yg, axis=-1, keepdims=True)
        y_ref[0, pl.ds(r0, L), :] = (yg * lax.rsqrt(ms + EPS) * nw_ref[...]).astype(BF16)
        return carry_unused

    lax.fori_loop(0, rows // L, chunk, 0)


def _ssd(p, sm, cwx, cbx, cwbc, cbbc, bias_row, alog_row, dskip_e, nw, tril, ee, eq, ek):
    b, s, _ = p.shape
    rows = SSD_ROWS
    const = lambda shape: pl.BlockSpec(shape, lambda bi, i: tuple(0 for _ in shape))
    out_sds = jax.ShapeDtypeStruct((b, s, SSM_INNER), BF16)
    return pl.pallas_call(
        _ssd_kernel,
        grid=(b, s // rows),
        in_specs=[
            pl.BlockSpec((1, rows, SSM_INNER), lambda bi, i: (bi, i, COL_Z_SSM // SSM_INNER)),
            pl.BlockSpec((1, rows, SSM_INNER), lambda bi, i: (bi, i, COL_XS // SSM_INNER)),
            pl.BlockSpec((1, rows, BC_W), lambda bi, i: (bi, i, COL_BC // BC_W)),
            pl.BlockSpec((1, rows, LANES), lambda bi, i: (bi, i, 0)),
            const((CONV_K, SSM_INNER)), const((1, SSM_INNER)),
            const((CONV_K, BC_W)), const((1, BC_W)),
            const((1, LANES)), const((1, LANES)),
            const((1, SSM_INNER)), const((1, SSM_INNER)),
            const((CHUNK, CHUNK)), const((2 * LANES, SSM_INNER)),
            const((3 * LANES, ATT_INNER)), const((3 * LANES, ATT_INNER)),
        ],
        out_specs=[
            pl.BlockSpec((1, rows, SSM_INNER), lambda bi, i: (bi, i, 0)),
            pl.BlockSpec((1, rows, ATT_INNER), lambda bi, i: (bi, i, 0)),
            pl.BlockSpec((1, rows, ATT_INNER), lambda bi, i: (bi, i, 0)),
        ],
        out_shape=[out_sds, out_sds, out_sds],
        scratch_shapes=[
            pltpu.VMEM((rows + TAIL, SSM_INNER), F32),
            pltpu.VMEM((rows + TAIL, BC_W), F32),
            pltpu.VMEM((rows, SSM_INNER), F32),
            pltpu.VMEM((rows, BC_W), F32),
            pltpu.VMEM((SSM_GROUPS, LANES, GROUP_W), F32),
            pltpu.VMEM((1, LANES), F32),
        ],
        compiler_params=pltpu.CompilerParams(
            dimension_semantics=("arbitrary", "arbitrary"), vmem_limit_bytes=VMEM_LIMIT),
        name="ssd",
    )(p, p, p, sm, cwx, cbx, cwbc, cbbc, bias_row, alog_row, dskip_e, nw, tril, ee, eq, ek)


ATT_TQ = 512
ATT_TK = 512


def _attn_kernel(q_ref, qa_ref, k_ref, ka_ref, v_ref, z_ref, o_ref, m_ref, l_ref, acc_ref):
    tq, tk = ATT_TQ, ATT_TK
    qi = pl.program_id(2)
    qq = jnp.concatenate([q_ref[0], qa_ref[0]], axis=1)
    m_ref[...] = jnp.full(m_ref.shape, -jnp.inf, F32)
    l_ref[...] = jnp.zeros(l_ref.shape, F32)
    acc_ref[...] = jnp.zeros(acc_ref.shape, F32)

    def step(j, masked):
        r0 = pl.multiple_of(j * tk, tk)
        kk = jnp.concatenate([k_ref[0, pl.ds(r0, tk), :], ka_ref[0, pl.ds(r0, tk), :]], axis=1)
        s = lax.dot_general(qq, kk, (((1,), (1,)), ((), ())), preferred_element_type=F32)
        if masked:
            row = lax.broadcasted_iota(jnp.int32, (tq, tk), 0)
            col = lax.broadcasted_iota(jnp.int32, (tq, tk), 1)
            s = jnp.where(row >= col, s, -jnp.inf)
        m_old = m_ref[...]
        m_new = jnp.maximum(m_old, jnp.max(s, axis=1, keepdims=True))
        alpha = jnp.exp(m_old - m_new)
        p = jnp.exp(s - m_new)
        l_ref[...] = alpha * l_ref[...] + jnp.sum(p, axis=1, keepdims=True)
        acc_ref[...] = alpha * acc_ref[...] + _dot(p.astype(BF16), v_ref[0, pl.ds(r0, tk), :])
        m_ref[...] = m_new

    def body(j, carry):
        step(j, False)
        return carry

    lax.fori_loop(0, qi, body, 0)
    step(qi, True)

    z = z_ref[0].astype(F32)
    o = acc_ref[...] / l_ref[...]
    o_ref[0] = (o * (z * jax.nn.sigmoid(z))).astype(BF16)


def _attn(p, qa, ka):
    b, s, _ = p.shape
    tq = ATT_TQ
    hd = ATT_HEADDIM
    return pl.pallas_call(
        _attn_kernel,
        grid=(b, ATT_HEADS, s // tq),
        in_specs=[
            pl.BlockSpec((1, tq, hd), lambda bi, h, i: (bi, i, COL_Q // hd + h)),
            pl.BlockSpec((1, tq, hd), lambda bi, h, i: (bi, i, h)),
            pl.BlockSpec((1, s, hd), lambda bi, h, i: (bi, 0, COL_K // hd + h)),
            pl.BlockSpec((1, s, hd), lambda bi, h, i: (bi, 0, h)),
            pl.BlockSpec((1, s, hd), lambda bi, h, i: (bi, 0, COL_V // hd + h)),
            pl.BlockSpec((1, tq, hd), lambda bi, h, i: (bi, i, COL_Z_ATT // hd + h)),
        ],
        out_specs=pl.BlockSpec((1, tq, hd), lambda bi, h, i: (bi, i, h)),
        out_shape=jax.ShapeDtypeStruct((b, s, ATT_INNER), BF16),
        scratch_shapes=[
            pltpu.VMEM((tq, 1), F32),
            pltpu.VMEM((tq, 1), F32),
            pltpu.VMEM((tq, hd), F32),
        ],
        compiler_params=pltpu.CompilerParams(
            dimension_semantics=("arbitrary", "arbitrary", "arbitrary"),
            vmem_limit_bytes=VMEM_LIMIT),
        name="fox_attn",
    )(p, qa, p, ka, p, p)


OUT_TM = 512


def _out_kernel(ys_ref, ya_ref, g_ref, x_ref, gate_ref, bg_ref, wps_ref, wpa_ref, wo_ref,
                fnw_ref, o_ref):
    g = jax.nn.sigmoid(g_ref[0].astype(F32) + bg_ref[...])
    ps = _dot(ys_ref[0], wps_ref[...])
    pa = _dot(ya_ref[0], wpa_ref[...])
    merged = g[:, 0:D_MODEL] * ps + g[:, D_MODEL:2 * D_MODEL] * pa
    out = _dot(merged.astype(BF16), wo_ref[...])
    xn = x_ref[0] + gate_ref[0] * out
    ms = jnp.mean(xn * xn, axis=-1, keepdims=True)
    o_ref[0] = xn * lax.rsqrt(ms + EPS) * fnw_ref[...]


def _outproj(y_ssm, y_att, p, x, gate, b_gate, wps, wpa, wo, fnw):
    b, s, d = x.shape
    tm = OUT_TM
    const = lambda shape: pl.BlockSpec(shape, lambda bi, i: tuple(0 for _ in shape))
    return pl.pallas_call(
        _out_kernel,
        grid=(b, s // tm),
        in_specs=[
            pl.BlockSpec((1, tm, d), lambda bi, i: (bi, i, 0)),
            pl.BlockSpec((1, tm, d), lambda bi, i: (bi, i, 0)),
            pl.BlockSpec((1, tm, 2 * d), lambda bi, i: (bi, i, COL_G // (2 * d))),
            pl.BlockSpec((1, tm, d), lambda bi, i: (bi, i, 0)),
            pl.BlockSpec((1, 1, d), lambda bi, i: (bi, 0, 0)),
            const((1, 2 * d)), const((d, d)), const((d, d)), const((d, d)), const((1, d)),
        ],
        out_specs=pl.BlockSpec((1, tm, d), lambda bi, i: (bi, i, 0)),
        out_shape=jax.ShapeDtypeStruct((b, s, d), F32),
        compiler_params=pltpu.CompilerParams(
            dimension_semantics=("arbitrary", "arbitrary"), vmem_limit_bytes=VMEM_LIMIT),
        name="outproj",
    )(y_ssm, y_att, p, x, gate, b_gate, wps, wpa, wo, fnw)


def _constants():
    tril = np.tril(np.ones((CHUNK, CHUNK), np.float32))
    ee = np.zeros((2 * LANES, SSM_INNER), np.float32)
    for h in range(SSM_HEADS):
        ee[h, h * SSM_HEADDIM:(h + 1) * SSM_HEADDIM] = 1.0
        ee[LANES + h, h * SSM_HEADDIM:(h + 1) * SSM_HEADDIM] = 1.0
    eq = np.zeros((3 * LANES, ATT_INNER), np.float32)
    ek = np.zeros((3 * LANES, ATT_INNER), np.float32)
    for h in range(ATT_HEADS):
        base = h * ATT_HEADDIM
        for part in range(3):
            eq[part * LANES + F_LANE0 + h, base + 3 + part] = 1.0
            ek[part * LANES + F_LANE0 + h, base + part] = -1.0
            eq[ONE_LANE, base + part] = 1.0
            ek[ONE_LANE, base + 3 + part] = 1.0
    as_bf16 = lambda a: jnp.asarray(a, dtype=BF16)
    return as_bf16(tril), as_bf16(ee), as_bf16(eq), as_bf16(ek)


def kernel(x, c, w_ada, b_ada, norm_w, w_in, conv_w, conv_b, dt_bias, a_log, d_skip,
           ssm_norm_w, b_f, b_gate, w_proj_ssm, w_proj_att, w_out, final_norm_w):
    b, s, d = x.shape
    row = lambda v: v.reshape(1, -1).astype(F32)

    c_pad = jnp.zeros((8, d), F32).at[0:b].set(c)
    ada = _ada(c_pad, w_ada, row(b_ada))[0:b]
    shift = ada[:, 0:d].reshape(b, 1, d)
    scale = ada[:, d:2 * d].reshape(b, 1, d)
    gate = ada[:, 2 * d:3 * d].reshape(b, 1, d)

    o_z, o_xbc, o_dt, o_q, o_k, o_v, o_za, o_f, o_g = np.cumsum(
        [0, SSM_INNER, SSM_INNER + BC_W, SSM_HEADS, ATT_INNER, ATT_INNER, ATT_INNER,
         ATT_INNER, ATT_HEADS]).tolist()
    w_big = jnp.concatenate([
        w_in[:, o_z:o_z + SSM_INNER],
        w_in[:, o_xbc:o_xbc + SSM_INNER],
        w_in[:, o_q:o_q + ATT_INNER],
        w_in[:, o_k:o_k + ATT_INNER],
        w_in[:, o_v:o_v + ATT_INNER],
        w_in[:, o_za:o_za + ATT_INNER],
        w_in[:, o_g:o_g + 2 * d],
        w_in[:, o_xbc + SSM_INNER:o_xbc + SSM_INNER + BC_W],
    ], axis=1).astype(BF16)
    w_sm = jnp.concatenate([
        w_in[:, o_dt:o_dt + SSM_HEADS], w_in[:, o_f:o_f + ATT_HEADS],
        jnp.zeros((d, LANES - SSM_HEADS - ATT_HEADS), F32)], axis=1)
    w_sm_hi = w_sm.astype(BF16)
    w_sm_lo = (w_sm - w_sm_hi.astype(F32)).astype(BF16)
    w_small = jnp.stack([w_sm_hi, w_sm_lo])
    colscale = np.ones((1, P_COLS), np.float32)
    colscale[:, COL_Q:COL_Q + ATT_INNER] = ATT_HEADDIM ** -0.5
    colscale = jnp.asarray(colscale)

    p, sm = _inproj(x, row(norm_w), scale, shift, w_big, w_small, colscale)

    tril, ee, eq, ek = _constants()
    pad_lanes = lambda v, lane0: jnp.zeros((1, LANES), F32).at[0, lane0:lane0 + v.shape[0]].set(v)
    bias_row = pad_lanes(dt_bias.astype(F32), DT_LANE0) + pad_lanes(b_f.astype(F32), F_LANE0)
    alog_row = pad_lanes(a_log.astype(F32), DT_LANE0)
    dskip_e = jnp.repeat(d_skip.astype(F32), SSM_HEADDIM).reshape(1, SSM_INNER)
    y_ssm, qa, ka = _ssd(
        p, sm,
        conv_w[:, 0:SSM_INNER].astype(F32), row(conv_b[0:SSM_INNER]),
        conv_w[:, SSM_INNER:].astype(F32), row(conv_b[SSM_INNER:]),
        bias_row, alog_row, dskip_e, row(ssm_norm_w), tril, ee, eq, ek)

    y_att = _attn(p, qa, ka)

    return _outproj(y_ssm, y_att, p, x, gate, row(b_gate),
                    w_proj_ssm.astype(BF16), w_proj_att.astype(BF16), w_out.astype(BF16),
                    row(final_norm_w))
```

```python
import functools

import jax
import jax.numpy as jnp
import numpy as np
from jax import lax
from jax.experimental import pallas as pl
from jax.experimental.pallas import tpu as pltpu

F32 = jnp.float32
BF16 = jnp.bfloat16

D_MODEL = 1024
SSM_HEADDIM = 64
SSM_HEADS = 16
SSM_INNER = SSM_HEADS * SSM_HEADDIM
SSM_GROUPS = 2
SSM_STATE = 64
CONV_K = 4
CHUNK = 128
ATT_HEADS = 8
ATT_HEADDIM = 128
ATT_INNER = ATT_HEADS * ATT_HEADDIM
EPS = 1e-6
LOG2E = 1.4426950408889634

LANES = 128
GROUP_W = SSM_INNER // SSM_GROUPS
BC_W = 2 * SSM_GROUPS * SSM_STATE

COL_Z_SSM = 0
COL_XS = 1024
COL_Q = 2048
COL_K = 3072
COL_V = 4096
COL_Z_ATT = 5120
COL_G = 6144
COL_BC = 8192
P_COLS = 8448

DT_LANE0 = 0
F_LANE0 = SSM_HEADS
ONE_LANE = LANES - 1

VMEM_LIMIT = 56 * 1024 * 1024


def _dot(a, b):
    return jnp.dot(a, b, preferred_element_type=F32)


def _split2(x):
    hi = x.astype(BF16)
    lo = (x - hi.astype(F32)).astype(BF16)
    return hi, lo


def _split3(x):
    hi = x.astype(BF16)
    r1 = x - hi.astype(F32)
    mid = r1.astype(BF16)
    lo = (r1 - mid.astype(F32)).astype(BF16)
    return hi, mid, lo


def _ada_kernel(c_ref, w_ref, b_ref, o_ref):
    c = c_ref[...]
    cs = c * jax.nn.sigmoid(c)
    w = w_ref[...]
    c_hi, c_lo = _split2(cs)
    w_hi, w_lo = _split2(w)
    o_ref[...] = _dot(c_hi, w_hi) + _dot(c_lo, w_hi) + _dot(c_hi, w_lo) + b_ref[...]


def _ada(c_pad, w_ada, b_ada):
    rows = c_pad.shape[0]
    n = w_ada.shape[1]
    tn = 1024
    return pl.pallas_call(
        _ada_kernel,
        grid=(n // tn,),
        in_specs=[
            pl.BlockSpec((rows, D_MODEL), lambda j: (0, 0)),
            pl.BlockSpec((D_MODEL, tn), lambda j: (0, j)),
            pl.BlockSpec((1, tn), lambda j: (0, j)),
        ],
        out_specs=pl.BlockSpec((rows, tn), lambda j: (0, j)),
        out_shape=jax.ShapeDtypeStruct((rows, n), F32),
        compiler_params=pltpu.CompilerParams(
            dimension_semantics=("arbitrary",), vmem_limit_bytes=VMEM_LIMIT),
        name="ada",
    )(c_pad, w_ada, b_ada)


INPROJ_TM = 1024
INPROJ_TN = 2816
INPROJ_CW = 256


def _inproj_kernel(x_ref, nw_ref, scale_ref, shift_ref, w_ref, ws_ref, cs_ref,
                   p_ref, sm_ref, h_ref):
    @pl.when(pl.program_id(2) == 0)
    def _():
        x = x_ref[0]
        ms = jnp.mean(x * x, axis=-1, keepdims=True)
        y = x * lax.rsqrt(ms + EPS) * nw_ref[...]
        h = y * (1.0 + scale_ref[0]) + shift_ref[0]
        h_hi, h_lo = _split2(h)
        h_ref[...] = h_hi
        sm_ref[0] = _dot(h_hi, ws_ref[0]) + _dot(h_lo, ws_ref[0]) + _dot(h_hi, ws_ref[1])

    h = h_ref[...]
    for c in range(INPROJ_TN // INPROJ_CW):
        sl = slice(c * INPROJ_CW, (c + 1) * INPROJ_CW)
        acc = _dot(h, w_ref[:, sl])
        p_ref[0, :, sl] = (acc * cs_ref[:, sl]).astype(BF16)


def _inproj(x, norm_w, scale, shift, w_big, w_small, colscale):
    b, s, d = x.shape
    tm, tn = INPROJ_TM, INPROJ_TN
    return pl.pallas_call(
        _inproj_kernel,
        grid=(b, s // tm, P_COLS // tn),
        in_specs=[
            pl.BlockSpec((1, tm, d), lambda bi, i, j: (bi, i, 0)),
            pl.BlockSpec((1, d), lambda bi, i, j: (0, 0)),
            pl.BlockSpec((1, 1, d), lambda bi, i, j: (bi, 0, 0)),
            pl.BlockSpec((1, 1, d), lambda bi, i, j: (bi, 0, 0)),
            pl.BlockSpec((d, tn), lambda bi, i, j: (0, j)),
            pl.BlockSpec((2, d, LANES), lambda bi, i, j: (0, 0, 0)),
            pl.BlockSpec((1, tn), lambda bi, i, j: (0, j)),
        ],
        out_specs=[
            pl.BlockSpec((1, tm, tn), lambda bi, i, j: (bi, i, j)),
            pl.BlockSpec((1, tm, LANES), lambda bi, i, j: (bi, i, 0)),
        ],
        out_shape=[
            jax.ShapeDtypeStruct((b, s, P_COLS), BF16),
            jax.ShapeDtypeStruct((b, s, LANES), F32),
        ],
        scratch_shapes=[pltpu.VMEM((tm, d), BF16)],
        compiler_params=pltpu.CompilerParams(
            dimension_semantics=("arbitrary", "arbitrary", "arbitrary"),
            vmem_limit_bytes=VMEM_LIMIT),
        name="inproj",
    )(x, norm_w, scale, shift, w_big, w_small, colscale)


SSD_ROWS = 512
TAIL = 8


def _ssd_kernel(z_ref, xs_ref, bc_ref, sm_ref, cwx_ref, cbx_ref, cwbc_ref, cbbc_ref,
                bias_ref, alog_ref, dskip_ref, nw_ref, tril_ref, ee_ref, eq_ref, ek_ref,
                y_ref, qa_ref, ka_ref,
                extx_ref, extbc_ref, xc_ref, bcc_ref, state_ref, carry_ref):
    rows = SSD_ROWS
    L = CHUNK

    @pl.when(pl.program_id(1) == 0)
    def _():
        extx_ref[0:TAIL, :] = jnp.zeros((TAIL, SSM_INNER), F32)
        extbc_ref[0:TAIL, :] = jnp.zeros((TAIL, BC_W), F32)
        state_ref[...] = jnp.zeros(state_ref.shape, F32)
        carry_ref[...] = jnp.zeros(carry_ref.shape, F32)

    extx_ref[TAIL:TAIL + rows, :] = xs_ref[0].astype(F32)
    extbc_ref[TAIL:TAIL + rows, :] = bc_ref[0].astype(F32)

    def conv_silu(ext_ref, w_ref, b_ref):
        acc = b_ref[...] + w_ref[CONV_K - 1:CONV_K, :] * ext_ref[TAIL:TAIL + rows, :]
        for k in range(CONV_K - 1):
            off = TAIL - (CONV_K - 1) + k
            acc = acc + w_ref[k:k + 1, :] * ext_ref[off:off + rows, :]
        return acc * jax.nn.sigmoid(acc)

    xc_ref[...] = conv_silu(extx_ref, cwx_ref, cbx_ref)
    bcc_ref[...] = conv_silu(extbc_ref, cwbc_ref, cbbc_ref)
    extx_ref[0:TAIL, :] = extx_ref[rows:rows + TAIL, :]
    extbc_ref[0:TAIL, :] = extbc_ref[rows:rows + TAIL, :]

    lane = lax.broadcasted_iota(jnp.int32, (L, LANES), 1)
    is_dt = lane < SSM_HEADS
    is_f = jnp.logical_and(lane >= F_LANE0, lane < F_LANE0 + ATT_HEADS)
    row_i = lax.broadcasted_iota(jnp.int32, (L, L), 0)
    col_i = lax.broadcasted_iota(jnp.int32, (L, L), 1)
    causal = row_i >= col_i
    neg_a = -jnp.exp(alog_ref[...])
    tril = tril_ref[...]

    def chunk(c, carry_unused):
        r0 = pl.multiple_of(c * L, L)
        v = sm_ref[0, pl.ds(r0, L), :] + bias_ref[...]
        t = jnp.log1p(jnp.exp(-jnp.abs(v)))
        dt = jnp.maximum(v, 0.0) + t
        lf = jnp.minimum(v, 0.0) - t
        val = jnp.where(is_dt, dt * neg_a, jnp.where(is_f, lf, 0.0))
        v_hi, v_mid, v_lo = _split3(val)
        cs3 = _dot(tril, jnp.concatenate([v_hi, v_mid, v_lo], axis=1))
        cs = cs3[:, 0:LANES] + cs3[:, LANES:2 * LANES] + cs3[:, 2 * LANES:3 * LANES]

        fc = jnp.where(is_f, cs + carry_ref[...], 0.0)
        carry_ref[...] = fc[L - 1:L, :]
        fa = jnp.where(lane == ONE_LANE, 1.0, fc * LOG2E)
        f_hi, f_mid, f_lo = _split3(fa)
        f3 = jnp.concatenate([f_hi, f_mid, f_lo], axis=1)
        qa_ref[0, pl.ds(r0, L), :] = _dot(f3, eq_ref[...]).astype(BF16)
        ka_ref[0, pl.ds(r0, L), :] = _dot(f3, ek_ref[...]).astype(BF16)

        a_cs = jnp.where(is_dt, cs, 0.0)
        a_last = a_cs[L - 1:L, :]
        dtm = jnp.where(is_dt, dt, 0.0)
        w2 = jnp.where(is_dt, jnp.exp(a_cs), 0.0)
        w1 = dtm * jnp.exp(a_last - a_cs)
        st_hi, st_lo = _split2(jnp.concatenate([dtm, w2, w1], axis=0))
        ex = _dot(jnp.concatenate([st_hi, st_lo], axis=1), ee_ref[...])
        dt_e = ex[0:L]
        w2_e = ex[L:2 * L]
        w1_e = ex[2 * L:3 * L]

        xsc = xc_ref[pl.ds(r0, L), :]
        bcv = bcc_ref[pl.ds(r0, L), :]
        b_all = bcv[:, 0:LANES]
        c_all = bcv[:, LANES:2 * LANES]
        b_t = b_all.T.astype(BF16)
        xd = xsc * dt_e
        xw1 = (xsc * w1_e).astype(BF16)
        a_cs_t = a_cs.T

        ys = []
        for g in range(SSM_GROUPS):
            gsl = slice(g * GROUP_W, (g + 1) * GROUP_W)
            in_g = jnp.logical_and(lane >= g * SSM_STATE, lane < (g + 1) * SSM_STATE)
            c_m = jnp.where(in_g, c_all, 0.0).astype(BF16)
            cb = _dot(c_m, b_t)
            s_g = state_ref[g]
            y_off = _dot(c_m, s_g.astype(BF16)) * w2_e[:, gsl]
            pieces = []
            for j in range(GROUP_W // LANES):
                h_a = g * (SSM_HEADS // SSM_GROUPS) + 2 * j
                xp = xd[:, g * GROUP_W + j * LANES:g * GROUP_W + (j + 1) * LANES]
                x_a = jnp.where(lane < SSM_HEADDIM, xp, 0.0).astype(BF16)
                x_b = jnp.where(lane >= SSM_HEADDIM, xp, 0.0).astype(BF16)

                def lmat(h):
                    seg = a_cs[:, h:h + 1] - a_cs_t[h:h + 1, :]
                    return (cb * jnp.exp(jnp.where(causal, seg, -jnp.inf))).astype(BF16)

                pieces.append(_dot(lmat(h_a), x_a) + _dot(lmat(h_a + 1), x_b))
            ys.append(jnp.concatenate(pieces, axis=1) + y_off)
            state_ref[g] = s_g * w2_e[L - 1:L, gsl] + _dot(b_t, xw1[:, gsl])

        y = jnp.concatenate(ys, axis=1) + dskip_ref[...] * xsc
        z = z_ref[0, pl.ds(r0, L), :].astype(F32)
        yg = y * (z * jax.nn.sigmoid(z))
        ms = jnp.mean(yg * yg, axis=-1, keepdims=True)
        y_ref[0, pl.ds(r0, L), :] = (yg * lax.rsqrt(ms + EPS) * nw_ref[...]).astype(BF16)
        return carry_unused

    lax.fori_loop(0, rows // L, chunk, 0)


def _ssd(p, sm, cwx, cbx, cwbc, cbbc, bias_row, alog_row, dskip_e, nw, tril, ee, eq, ek):
    b, s, _ = p.shape
    rows = SSD_ROWS
    const = lambda shape: pl.BlockSpec(shape, lambda bi, i: tuple(0 for _ in shape))
    out_sds = jax.ShapeDtypeStruct((b, s, SSM_INNER), BF16)
    return pl.pallas_call(
        _ssd_kernel,
        grid=(b, s // rows),
        in_specs=[
            pl.BlockSpec((1, rows, SSM_INNER), lambda bi, i: (bi, i, COL_Z_SSM // SSM_INNER)),
            pl.BlockSpec((1, rows, SSM_INNER), lambda bi, i: (bi, i, COL_XS // SSM_INNER)),
            pl.BlockSpec((1, rows, BC_W), lambda bi, i: (bi, i, COL_BC // BC_W)),
            pl.BlockSpec((1, rows, LANES), lambda bi, i: (bi, i, 0)),
            const((CONV_K, SSM_INNER)), const((1, SSM_INNER)),
            const((CONV_K, BC_W)), const((1, BC_W)),
            const((1, LANES)), const((1, LANES)),
            const((1, SSM_INNER)), const((1, SSM_INNER)),
            const((CHUNK, CHUNK)), const((2 * LANES, SSM_INNER)),
            const((3 * LANES, ATT_INNER)), const((3 * LANES, ATT_INNER)),
        ],
        out_specs=[
            pl.BlockSpec((1, rows, SSM_INNER), lambda bi, i: (bi, i, 0)),
            pl.BlockSpec((1, rows, ATT_INNER), lambda bi, i: (bi, i, 0)),
            pl.BlockSpec((1, rows, ATT_INNER), lambda bi, i: (bi, i, 0)),
        ],
        out_shape=[out_sds, out_sds, out_sds],
        scratch_shapes=[
            pltpu.VMEM((rows + TAIL, SSM_INNER), F32),
            pltpu.VMEM((rows + TAIL, BC_W), F32),
            pltpu.VMEM((rows, SSM_INNER), F32),
            pltpu.VMEM((rows, BC_W), F32),
            pltpu.VMEM((SSM_GROUPS, LANES, GROUP_W), F32),
            pltpu.VMEM((1, LANES), F32),
        ],
        compiler_params=pltpu.CompilerParams(
            dimension_semantics=("arbitrary", "arbitrary"), vmem_limit_bytes=VMEM_LIMIT),
        name="ssd",
    )(p, p, p, sm, cwx, cbx, cwbc, cbbc, bias_row, alog_row, dskip_e, nw, tril, ee, eq, ek)


ATT_TQ = 512
ATT_TK = 512


def _attn_kernel(q_ref, qa_ref, k_ref, ka_ref, v_ref, z_ref, o_ref,
                 s0_ref, s1_ref, mx0_ref, mx1_ref, m_ref, l_ref, acc_ref):
    tq, tk = ATT_TQ, ATT_TK
    qi = pl.program_id(2)
    qq = jnp.concatenate([q_ref[0], qa_ref[0]], axis=1)
    m_ref[...] = jnp.full(m_ref.shape, -jnp.inf, F32)
    l_ref[...] = jnp.zeros(l_ref.shape, F32)
    acc_ref[...] = jnp.zeros(acc_ref.shape, F32)

    def scores(j, s_ref, mx_ref):
        r0 = pl.multiple_of(j * tk, tk)
        kk = jnp.concatenate([k_ref[0, pl.ds(r0, tk), :], ka_ref[0, pl.ds(r0, tk), :]], axis=1)
        s = lax.dot_general(qq, kk, (((1,), (1,)), ((), ())), preferred_element_type=F32)
        s_ref[...] = s
        mx_ref[...] = jnp.broadcast_to(jnp.max(s, axis=1, keepdims=True), (tq, LANES))

    def update(j, s_ref, mx_ref, masked):
        r0 = pl.multiple_of(j * tk, tk)
        s = s_ref[...]
        if masked:
            row = lax.broadcasted_iota(jnp.int32, (tq, tk), 0)
            col = lax.broadcasted_iota(jnp.int32, (tq, tk), 1)
            s = jnp.where(row >= col, s, -jnp.inf)
            mx = jnp.broadcast_to(jnp.max(s, axis=1, keepdims=True), (tq, LANES))
        else:
            mx = mx_ref[...]
        m_old = m_ref[...]
        m_new = jnp.maximum(m_old, mx)
        alpha = jnp.exp2(m_old - m_new)
        ps = [jnp.exp2(s[:, c * LANES:(c + 1) * LANES] - m_new) for c in range(tk // LANES)]
        l_ref[...] = alpha * l_ref[...] + functools.reduce(lambda a, b: a + b, ps)
        p = jnp.concatenate([x.astype(BF16) for x in ps], axis=1)
        acc_ref[...] = alpha * acc_ref[...] + _dot(p, v_ref[0, pl.ds(r0, tk), :])
        m_ref[...] = m_new

    scores(0, s0_ref, mx0_ref)

    def body(i, carry):
        j = 2 * i
        scores(j + 1, s1_ref, mx1_ref)
        update(j, s0_ref, mx0_ref, False)
        scores(j + 2, s0_ref, mx0_ref)
        update(j + 1, s1_ref, mx1_ref, False)
        return carry

    lax.fori_loop(0, qi // 2, body, 0)

    @pl.when(qi % 2 == 0)
    def _():
        update(qi, s0_ref, mx0_ref, True)

    @pl.when(qi % 2 == 1)
    def _():
        scores(qi, s1_ref, mx1_ref)
        update(qi - 1, s0_ref, mx0_ref, False)
        update(qi, s1_ref, mx1_ref, True)

    z = z_ref[0].astype(F32)
    o = acc_ref[...] / jnp.sum(l_ref[...], axis=1, keepdims=True)
    o_ref[0] = (o * (z * jax.nn.sigmoid(z))).astype(BF16)


def _attn(p, qa, ka):
    b, s, _ = p.shape
    tq = ATT_TQ
    hd = ATT_HEADDIM
    return pl.pallas_call(
        _attn_kernel,
        grid=(b, ATT_HEADS, s // tq),
        in_specs=[
            pl.BlockSpec((1, tq, hd), lambda bi, h, i: (bi, i, COL_Q // hd + h)),
            pl.BlockSpec((1, tq, hd), lambda bi, h, i: (bi, i, h)),
            pl.BlockSpec((1, s, hd), lambda bi, h, i: (bi, 0, COL_K // hd + h)),
            pl.BlockSpec((1, s, hd), lambda bi, h, i: (bi, 0, h)),
            pl.BlockSpec((1, s, hd), lambda bi, h, i: (bi, 0, COL_V // hd + h)),
            pl.BlockSpec((1, tq, hd), lambda bi, h, i: (bi, i, COL_Z_ATT // hd + h)),
        ],
        out_specs=pl.BlockSpec((1, tq, hd), lambda bi, h, i: (bi, i, h)),
        out_shape=jax.ShapeDtypeStruct((b, s, ATT_INNER), BF16),
        scratch_shapes=[
            pltpu.VMEM((tq, ATT_TK), F32),
            pltpu.VMEM((tq, ATT_TK), F32),
            pltpu.VMEM((tq, LANES), F32),
            pltpu.VMEM((tq, LANES), F32),
            pltpu.VMEM((tq, LANES), F32),
            pltpu.VMEM((tq, LANES), F32),
            pltpu.VMEM((tq, hd), F32),
        ],
        compiler_params=pltpu.CompilerParams(
            dimension_semantics=("arbitrary", "arbitrary", "arbitrary"),
            vmem_limit_bytes=VMEM_LIMIT),
        name="fox_attn",
    )(p, qa, p, ka, p, p)


OUT_TM = 512


def _out_kernel(ys_ref, ya_ref, g_ref, x_ref, gate_ref, bg_ref, wps_ref, wpa_ref, wo_ref,
                fnw_ref, o_ref):
    g = jax.nn.sigmoid(g_ref[0].astype(F32) + bg_ref[...])
    ps = _dot(ys_ref[0], wps_ref[...])
    pa = _dot(ya_ref[0], wpa_ref[...])
    merged = g[:, 0:D_MODEL] * ps + g[:, D_MODEL:2 * D_MODEL] * pa
    out = _dot(merged.astype(BF16), wo_ref[...])
    xn = x_ref[0] + gate_ref[0] * out
    ms = jnp.mean(xn * xn, axis=-1, keepdims=True)
    o_ref[0] = xn * lax.rsqrt(ms + EPS) * fnw_ref[...]


def _outproj(y_ssm, y_att, p, x, gate, b_gate, wps, wpa, wo, fnw):
    b, s, d = x.shape
    tm = OUT_TM
    const = lambda shape: pl.BlockSpec(shape, lambda bi, i: tuple(0 for _ in shape))
    return pl.pallas_call(
        _out_kernel,
        grid=(b, s // tm),
        in_specs=[
            pl.BlockSpec((1, tm, d), lambda bi, i: (bi, i, 0)),
            pl.BlockSpec((1, tm, d), lambda bi, i: (bi, i, 0)),
            pl.BlockSpec((1, tm, 2 * d), lambda bi, i: (bi, i, COL_G // (2 * d))),
            pl.BlockSpec((1, tm, d), lambda bi, i: (bi, i, 0)),
            pl.BlockSpec((1, 1, d), lambda bi, i: (bi, 0, 0)),
            const((1, 2 * d)), const((d, d)), const((d, d)), const((d, d)), const((1, d)),
        ],
        out_specs=pl.BlockSpec((1, tm, d), lambda bi, i: (bi, i, 0)),
        out_shape=jax.ShapeDtypeStruct((b, s, d), F32),
        compiler_params=pltpu.CompilerParams(
            dimension_semantics=("arbitrary", "arbitrary"), vmem_limit_bytes=VMEM_LIMIT),
        name="outproj",
    )(y_ssm, y_att, p, x, gate, b_gate, wps, wpa, wo, fnw)


def _constants():
    tril = np.tril(np.ones((CHUNK, CHUNK), np.float32))
    ee = np.zeros((2 * LANES, SSM_INNER), np.float32)
    for h in range(SSM_HEADS):
        ee[h, h * SSM_HEADDIM:(h + 1) * SSM_HEADDIM] = 1.0
        ee[LANES + h, h * SSM_HEADDIM:(h + 1) * SSM_HEADDIM] = 1.0
    eq = np.zeros((3 * LANES, ATT_INNER), np.float32)
    ek = np.zeros((3 * LANES, ATT_INNER), np.float32)
    for h in range(ATT_HEADS):
        base = h * ATT_HEADDIM
        for part in range(3):
            eq[part * LANES + F_LANE0 + h, base + 3 + part] = 1.0
            ek[part * LANES + F_LANE0 + h, base + part] = -1.0
            eq[ONE_LANE, base + part] = 1.0
            ek[ONE_LANE, base + 3 + part] = 1.0
    as_bf16 = lambda a: jnp.asarray(a, dtype=BF16)
    return as_bf16(tril), as_bf16(ee), as_bf16(eq), as_bf16(ek)


def kernel(x, c, w_ada, b_ada, norm_w, w_in, conv_w, conv_b, dt_bias, a_log, d_skip,
           ssm_norm_w, b_f, b_gate, w_proj_ssm, w_proj_att, w_out, final_norm_w):
    b, s, d = x.shape
    row = lambda v: v.reshape(1, -1).astype(F32)

    c_pad = jnp.zeros((8, d), F32).at[0:b].set(c)
    ada = _ada(c_pad, w_ada, row(b_ada))[0:b]
    shift = ada[:, 0:d].reshape(b, 1, d)
    scale = ada[:, d:2 * d].reshape(b, 1, d)
    gate = ada[:, 2 * d:3 * d].reshape(b, 1, d)

    o_z, o_xbc, o_dt, o_q, o_k, o_v, o_za, o_f, o_g = np.cumsum(
        [0, SSM_INNER, SSM_INNER + BC_W, SSM_HEADS, ATT_INNER, ATT_INNER, ATT_INNER,
         ATT_INNER, ATT_HEADS]).tolist()
    w_big = jnp.concatenate([
        w_in[:, o_z:o_z + SSM_INNER],
        w_in[:, o_xbc:o_xbc + SSM_INNER],
        w_in[:, o_q:o_q + ATT_INNER],
        w_in[:, o_k:o_k + ATT_INNER],
        w_in[:, o_v:o_v + ATT_INNER],
        w_in[:, o_za:o_za + ATT_INNER],
        w_in[:, o_g:o_g + 2 * d],
        w_in[:, o_xbc + SSM_INNER:o_xbc + SSM_INNER + BC_W],
    ], axis=1).astype(BF16)
    w_sm = jnp.concatenate([
        w_in[:, o_dt:o_dt + SSM_HEADS], w_in[:, o_f:o_f + ATT_HEADS],
        jnp.zeros((d, LANES - SSM_HEADS - ATT_HEADS), F32)], axis=1)
    w_sm_hi = w_sm.astype(BF16)
    w_sm_lo = (w_sm - w_sm_hi.astype(F32)).astype(BF16)
    w_small = jnp.stack([w_sm_hi, w_sm_lo])
    colscale = np.ones((1, P_COLS), np.float32)
    colscale[:, COL_Q:COL_Q + ATT_INNER] = ATT_HEADDIM ** -0.5 * LOG2E
    colscale = jnp.asarray(colscale)

    p, sm = _inproj(x, row(norm_w), scale, shift, w_big, w_small, colscale)

    tril, ee, eq, ek = _constants()
    pad_lanes = lambda v, lane0: jnp.zeros((1, LANES), F32).at[0, lane0:lane0 + v.shape[0]].set(v)
    bias_row = pad_lanes(dt_bias.astype(F32), DT_LANE0) + pad_lanes(b_f.astype(F32), F_LANE0)
    alog_row = pad_lanes(a_log.astype(F32), DT_LANE0)
    dskip_e = jnp.repeat(d_skip.astype(F32), SSM_HEADDIM).reshape(1, SSM_INNER)
    y_ssm, qa, ka = _ssd(
        p, sm,
        conv_w[:, 0:SSM_INNER].astype(F32), row(conv_b[0:SSM_INNER]),
        conv_w[:, SSM_INNER:].astype(F32), row(conv_b[SSM_INNER:]),
        bias_row, alog_row, dskip_e, row(ssm_norm_w), tril, ee, eq, ek)

    y_att = _attn(p, qa, ka)

    return _outproj(y_ssm, y_att, p, x, gate, row(b_gate),
                    w_proj_ssm.astype(BF16), w_proj_att.astype(BF16), w_out.astype(BF16),
                    row(final_norm_w))
```

```python
import functools

import jax
import jax.numpy as jnp
import numpy as np
from jax import lax
from jax.experimental import pallas as pl
from jax.experimental.pallas import tpu as pltpu

F32 = jnp.float32
BF16 = jnp.bfloat16

D_MODEL = 1024
SSM_HEADDIM = 64
SSM_HEADS = 16
SSM_INNER = SSM_HEADS * SSM_HEADDIM
SSM_GROUPS = 2
SSM_STATE = 64
CONV_K = 4
CHUNK = 128
ATT_HEADS = 8
ATT_HEADDIM = 128
ATT_INNER = ATT_HEADS * ATT_HEADDIM
EPS = 1e-6
LOG2E = 1.4426950408889634

LANES = 128
GROUP_W = SSM_INNER // SSM_GROUPS
BC_W = 2 * SSM_GROUPS * SSM_STATE

COL_Z_SSM = 0
COL_XS = 1024
COL_Q = 2048
COL_K = 3072
COL_V = 4096
COL_Z_ATT = 5120
COL_G = 6144
COL_BC = 8192
P_COLS = 8448

DT_LANE0 = 0
F_LANE0 = SSM_HEADS
ONE_LANE = LANES - 1
AUG_W = LANES // ATT_HEADS

VMEM_LIMIT = 56 * 1024 * 1024


def _dot(a, b):
    return jnp.dot(a, b, preferred_element_type=F32)


def _split2(x):
    hi = x.astype(BF16)
    lo = (x - hi.astype(F32)).astype(BF16)
    return hi, lo


def _split3(x):
    hi = x.astype(BF16)
    r1 = x - hi.astype(F32)
    mid = r1.astype(BF16)
    lo = (r1 - mid.astype(F32)).astype(BF16)
    return hi, mid, lo


def _ada_kernel(c_ref, w_ref, b_ref, o_ref):
    c = c_ref[...]
    cs = c * jax.nn.sigmoid(c)
    w = w_ref[...]
    c_hi, c_lo = _split2(cs)
    w_hi, w_lo = _split2(w)
    o_ref[...] = _dot(c_hi, w_hi) + _dot(c_lo, w_hi) + _dot(c_hi, w_lo) + b_ref[...]


def _ada(c_pad, w_ada, b_ada):
    rows = c_pad.shape[0]
    n = w_ada.shape[1]
    tn = 1024
    return pl.pallas_call(
        _ada_kernel,
        grid=(n // tn,),
        in_specs=[
            pl.BlockSpec((rows, D_MODEL), lambda j: (0, 0)),
            pl.BlockSpec((D_MODEL, tn), lambda j: (0, j)),
            pl.BlockSpec((1, tn), lambda j: (0, j)),
        ],
        out_specs=pl.BlockSpec((rows, tn), lambda j: (0, j)),
        out_shape=jax.ShapeDtypeStruct((rows, n), F32),
        compiler_params=pltpu.CompilerParams(
            dimension_semantics=("arbitrary",), vmem_limit_bytes=VMEM_LIMIT),
        name="ada",
    )(c_pad, w_ada, b_ada)


INPROJ_TM = 1024
INPROJ_TN = 2816
INPROJ_CW = 256


def _inproj_kernel(x_ref, nw_ref, scale_ref, shift_ref, w_ref, ws_ref, cs_ref,
                   p_ref, sm_ref, h_ref):
    @pl.when(pl.program_id(2) == 0)
    def _():
        x = x_ref[0]
        ms = jnp.mean(x * x, axis=-1, keepdims=True)
        y = x * lax.rsqrt(ms + EPS) * nw_ref[...]
        h = y * (1.0 + scale_ref[0]) + shift_ref[0]
        h_hi, h_lo = _split2(h)
        h_ref[...] = h_hi
        sm_ref[0] = _dot(h_hi, ws_ref[0]) + _dot(h_lo, ws_ref[0]) + _dot(h_hi, ws_ref[1])

    h = h_ref[...]
    for c in range(INPROJ_TN // INPROJ_CW):
        sl = slice(c * INPROJ_CW, (c + 1) * INPROJ_CW)
        acc = _dot(h, w_ref[:, sl])
        p_ref[0, :, sl] = (acc * cs_ref[:, sl]).astype(BF16)


def _inproj(x, norm_w, scale, shift, w_big, w_small, colscale):
    b, s, d = x.shape
    tm, tn = INPROJ_TM, INPROJ_TN
    return pl.pallas_call(
        _inproj_kernel,
        grid=(b, s // tm, P_COLS // tn),
        in_specs=[
            pl.BlockSpec((1, tm, d), lambda bi, i, j: (bi, i, 0)),
            pl.BlockSpec((1, d), lambda bi, i, j: (0, 0)),
            pl.BlockSpec((1, 1, d), lambda bi, i, j: (bi, 0, 0)),
            pl.BlockSpec((1, 1, d), lambda bi, i, j: (bi, 0, 0)),
            pl.BlockSpec((d, tn), lambda bi, i, j: (0, j)),
            pl.BlockSpec((2, d, LANES), lambda bi, i, j: (0, 0, 0)),
            pl.BlockSpec((1, tn), lambda bi, i, j: (0, j)),
        ],
        out_specs=[
            pl.BlockSpec((1, tm, tn), lambda bi, i, j: (bi, i, j)),
            pl.BlockSpec((1, tm, LANES), lambda bi, i, j: (bi, i, 0)),
        ],
        out_shape=[
            jax.ShapeDtypeStruct((b, s, P_COLS), BF16),
            jax.ShapeDtypeStruct((b, s, LANES), F32),
        ],
        scratch_shapes=[pltpu.VMEM((tm, d), BF16)],
        compiler_params=pltpu.CompilerParams(
            dimension_semantics=("arbitrary", "arbitrary", "arbitrary"),
            vmem_limit_bytes=VMEM_LIMIT),
        name="inproj",
    )(x, norm_w, scale, shift, w_big, w_small, colscale)


SSD_ROWS = 512
TAIL = 8


def _ssd_kernel(z_ref, xs_ref, bc_ref, sm_ref, cwx_ref, cbx_ref, cwbc_ref, cbbc_ref,
                bias_ref, alog_ref, dskip_ref, nw_ref, tril_ref, ee_ref, eq_ref, ek_ref,
                y_ref, qa_ref, ka_ref,
                extx_ref, extbc_ref, xc_ref, bcc_ref, state_ref, carry_ref):
    rows = SSD_ROWS
    L = CHUNK

    @pl.when(pl.program_id(1) == 0)
    def _():
        extx_ref[0:TAIL, :] = jnp.zeros((TAIL, SSM_INNER), F32)
        extbc_ref[0:TAIL, :] = jnp.zeros((TAIL, BC_W), F32)
        state_ref[...] = jnp.zeros(state_ref.shape, F32)
        carry_ref[...] = jnp.zeros(carry_ref.shape, F32)

    extx_ref[TAIL:TAIL + rows, :] = xs_ref[0].astype(F32)
    extbc_ref[TAIL:TAIL + rows, :] = bc_ref[0].astype(F32)

    def conv_silu(ext_ref, w_ref, b_ref):
        acc = b_ref[...] + w_ref[CONV_K - 1:CONV_K, :] * ext_ref[TAIL:TAIL + rows, :]
        for k in range(CONV_K - 1):
            off = TAIL - (CONV_K - 1) + k
            acc = acc + w_ref[k:k + 1, :] * ext_ref[off:off + rows, :]
        return acc * jax.nn.sigmoid(acc)

    xc_ref[...] = conv_silu(extx_ref, cwx_ref, cbx_ref)
    bcc_ref[...] = conv_silu(extbc_ref, cwbc_ref, cbbc_ref)
    extx_ref[0:TAIL, :] = extx_ref[rows:rows + TAIL, :]
    extbc_ref[0:TAIL, :] = extbc_ref[rows:rows + TAIL, :]

    lane = lax.broadcasted_iota(jnp.int32, (L, LANES), 1)
    is_dt = lane < SSM_HEADS
    is_f = jnp.logical_and(lane >= F_LANE0, lane < F_LANE0 + ATT_HEADS)
    row_i = lax.broadcasted_iota(jnp.int32, (L, L), 0)
    col_i = lax.broadcasted_iota(jnp.int32, (L, L), 1)
    causal = row_i >= col_i
    neg_a = -jnp.exp(alog_ref[...])
    tril = tril_ref[...]

    def chunk(c, carry_unused):
        r0 = pl.multiple_of(c * L, L)
        v = sm_ref[0, pl.ds(r0, L), :] + bias_ref[...]
        t = jnp.log1p(jnp.exp(-jnp.abs(v)))
        dt = jnp.maximum(v, 0.0) + t
        lf = jnp.minimum(v, 0.0) - t
        val = jnp.where(is_dt, dt * neg_a, jnp.where(is_f, lf, 0.0))
        v_hi, v_mid, v_lo = _split3(val)
        cs3 = _dot(tril, jnp.concatenate([v_hi, v_mid, v_lo], axis=1))
        cs = cs3[:, 0:LANES] + cs3[:, LANES:2 * LANES] + cs3[:, 2 * LANES:3 * LANES]

        fc = jnp.where(is_f, cs + carry_ref[...], 0.0)
        carry_ref[...] = fc[L - 1:L, :]
        fa = jnp.where(lane == ONE_LANE, 1.0, fc * LOG2E)
        f_hi, f_mid, f_lo = _split3(fa)
        f3 = jnp.concatenate([f_hi, f_mid, f_lo], axis=1)
        qa_ref[0, pl.ds(r0, L), :] = _dot(f3, eq_ref[...]).astype(BF16)
        ka_ref[0, pl.ds(r0, L), :] = _dot(f3, ek_ref[...]).astype(BF16)

        a_cs = jnp.where(is_dt, cs, 0.0)
        a_last = a_cs[L - 1:L, :]
        dtm = jnp.where(is_dt, dt, 0.0)
        w2 = jnp.where(is_dt, jnp.exp(a_cs), 0.0)
        w1 = dtm * jnp.exp(a_last - a_cs)
        st_hi, st_lo = _split2(jnp.concatenate([dtm, w2, w1], axis=0))
        ex = _dot(jnp.concatenate([st_hi, st_lo], axis=1), ee_ref[...])
        dt_e = ex[0:L]
        w2_e = ex[L:2 * L]
        w1_e = ex[2 * L:3 * L]

        xsc = xc_ref[pl.ds(r0, L), :]
        bcv = bcc_ref[pl.ds(r0, L), :]
        b_all = bcv[:, 0:LANES]
        c_all = bcv[:, LANES:2 * LANES]
        b_t = b_all.T.astype(BF16)
        xd = xsc * dt_e
        xw1 = (xsc * w1_e).astype(BF16)
        a_cs_t = a_cs.T

        ys = []
        for g in range(SSM_GROUPS):
            gsl = slice(g * GROUP_W, (g + 1) * GROUP_W)
            in_g = jnp.logical_and(lane >= g * SSM_STATE, lane < (g + 1) * SSM_STATE)
            c_m = jnp.where(in_g, c_all, 0.0).astype(BF16)
            cb = _dot(c_m, b_t)
            s_g = state_ref[g]
            y_off = _dot(c_m, s_g.astype(BF16)) * w2_e[:, gsl]
            pieces = []
            for j in range(GROUP_W // LANES):
                h_a = g * (SSM_HEADS // SSM_GROUPS) + 2 * j
                xp = xd[:, g * GROUP_W + j * LANES:g * GROUP_W + (j + 1) * LANES]
                x_a = jnp.where(lane < SSM_HEADDIM, xp, 0.0).astype(BF16)
                x_b = jnp.where(lane >= SSM_HEADDIM, xp, 0.0).astype(BF16)

                def lmat(h):
                    seg = a_cs[:, h:h + 1] - a_cs_t[h:h + 1, :]
                    return (cb * jnp.exp(jnp.where(causal, seg, -jnp.inf))).astype(BF16)

                pieces.append(_dot(lmat(h_a), x_a) + _dot(lmat(h_a + 1), x_b))
            ys.append(jnp.concatenate(pieces, axis=1) + y_off)
            state_ref[g] = s_g * w2_e[L - 1:L, gsl] + _dot(b_t, xw1[:, gsl])

        y = jnp.concatenate(ys, axis=1) + dskip_ref[...] * xsc
        z = z_ref[0, pl.ds(r0, L), :].astype(F32)
        yg = y * (z * jax.nn.sigmoid(z))
        ms = jnp.mean(yg * yg, axis=-1, keepdims=True)
        y_ref[0, pl.ds(r0, L), :] = (yg * lax.rsqrt(ms + EPS) * nw_ref[...]).astype(BF16)
        return carry_unused

    lax.fori_loop(0, rows // L, chunk, 0)


def _ssd(p, sm, cwx, cbx, cwbc, cbbc, bias_row, alog_row, dskip_e, nw, tril, ee, eq, ek):
    b, s, _ = p.shape
    rows = SSD_ROWS
    const = lambda shape: pl.BlockSpec(shape, lambda bi, i: tuple(0 for _ in shape))
    out_sds = jax.ShapeDtypeStruct((b, s, SSM_INNER), BF16)
    aug_sds = jax.ShapeDtypeStruct((b, s, LANES), BF16)
    return pl.pallas_call(
        _ssd_kernel,
        grid=(b, s // rows),
        in_specs=[
            pl.BlockSpec((1, rows, SSM_INNER), lambda bi, i: (bi, i, COL_Z_SSM // SSM_INNER)),
            pl.BlockSpec((1, rows, SSM_INNER), lambda bi, i: (bi, i, COL_XS // SSM_INNER)),
            pl.BlockSpec((1, rows, BC_W), lambda bi, i: (bi, i, COL_BC // BC_W)),
            pl.BlockSpec((1, rows, LANES), lambda bi, i: (bi, i, 0)),
            const((CONV_K, SSM_INNER)), const((1, SSM_INNER)),
            const((CONV_K, BC_W)), const((1, BC_W)),
            const((1, LANES)), const((1, LANES)),
            const((1, SSM_INNER)), const((1, SSM_INNER)),
            const((CHUNK, CHUNK)), const((2 * LANES, SSM_INNER)),
            const((3 * LANES, LANES)), const((3 * LANES, LANES)),
        ],
        out_specs=[
            pl.BlockSpec((1, rows, SSM_INNER), lambda bi, i: (bi, i, 0)),
            pl.BlockSpec((1, rows, LANES), lambda bi, i: (bi, i, 0)),
            pl.BlockSpec((1, rows, LANES), lambda bi, i: (bi, i, 0)),
        ],
        out_shape=[out_sds, aug_sds, aug_sds],
        scratch_shapes=[
            pltpu.VMEM((rows + TAIL, SSM_INNER), F32),
            pltpu.VMEM((rows + TAIL, BC_W), F32),
            pltpu.VMEM((rows, SSM_INNER), F32),
            pltpu.VMEM((rows, BC_W), F32),
            pltpu.VMEM((SSM_GROUPS, LANES, GROUP_W), F32),
            pltpu.VMEM((1, LANES), F32),
        ],
        compiler_params=pltpu.CompilerParams(
            dimension_semantics=("arbitrary", "arbitrary"), vmem_limit_bytes=VMEM_LIMIT),
        name="ssd",
    )(p, p, p, sm, cwx, cbx, cwbc, cbbc, bias_row, alog_row, dskip_e, nw, tril, ee, eq, ek)


ATT_TQ = 512
ATT_TK = 512
ATT_HPS = 2


def _attn_kernel(q_ref, qa_ref, k_ref, ka_ref, v_ref, z_ref, o_ref,
                 kam_ref, s_ref, mx_ref, m_ref, l_ref, acc_ref):
    tq, tk, hd = ATT_TQ, ATT_TK, ATT_HEADDIM
    qi = pl.program_id(2)
    heads = range(ATT_HPS)
    hsl = lambda hh: slice(hh * hd, (hh + 1) * hd)

    @pl.when(qi == 0)
    def _():
        lane_head = lax.broadcasted_iota(jnp.int32, (tk, LANES), 1) // AUG_W
        for hh in heads:
            own = lane_head == pl.program_id(1) * ATT_HPS + hh

            def fill(t, carry):
                r0 = pl.multiple_of(t * tk, tk)
                blk = ka_ref[0, pl.ds(r0, tk), :].astype(F32)
                kam_ref[hh, pl.ds(r0, tk), :] = jnp.where(own, blk, 0.0).astype(BF16)
                return carry

            lax.fori_loop(0, ka_ref.shape[1] // tk, fill, 0)

    qq = [jnp.concatenate([q_ref[0, :, hsl(hh)], qa_ref[0]], axis=1) for hh in heads]
    m_ref[...] = jnp.full(m_ref.shape, -jnp.inf, F32)
    l_ref[...] = jnp.zeros(l_ref.shape, F32)
    acc_ref[...] = jnp.zeros(acc_ref.shape, F32)

    def scores(j, slot):
        r0 = pl.multiple_of(j * tk, tk)
        for hh in heads:
            kk = jnp.concatenate(
                [k_ref[0, pl.ds(r0, tk), hsl(hh)], kam_ref[hh, pl.ds(r0, tk), :]], axis=1)
            s = lax.dot_general(qq[hh], kk, (((1,), (1,)), ((), ())), preferred_element_type=F32)
            s_ref[slot, hh] = s
            mx_ref[slot, hh] = jnp.broadcast_to(jnp.max(s, axis=1, keepdims=True), (tq, LANES))

    def update(j, slot, masked):
        r0 = pl.multiple_of(j * tk, tk)
        for hh in heads:
            s = s_ref[slot, hh]
            if masked:
                row = lax.broadcasted_iota(jnp.int32, (tq, tk), 0)
                col = lax.broadcasted_iota(jnp.int32, (tq, tk), 1)
                s = jnp.where(row >= col, s, -jnp.inf)
                mx = jnp.broadcast_to(jnp.max(s, axis=1, keepdims=True), (tq, LANES))
            else:
                mx = mx_ref[slot, hh]
            m_old = m_ref[hh]
            m_new = jnp.maximum(m_old, mx)
            alpha = jnp.exp2(m_old - m_new)
            ps = [jnp.exp2(s[:, c * LANES:(c + 1) * LANES] - m_new) for c in range(tk // LANES)]
            l_ref[hh] = alpha * l_ref[hh] + functools.reduce(lambda a, b: a + b, ps)
            p = jnp.concatenate([x.astype(BF16) for x in ps], axis=1)
            acc_ref[hh] = alpha * acc_ref[hh] + _dot(p, v_ref[0, pl.ds(r0, tk), hsl(hh)])
            m_ref[hh] = m_new

    scores(0, 0)

    def body(i, carry):
        j = 2 * i
        scores(j + 1, 1)
        update(j, 0, False)
        scores(j + 2, 0)
        update(j + 1, 1, False)
        return carry

    lax.fori_loop(0, qi // 2, body, 0)

    @pl.when(qi % 2 == 0)
    def _():
        update(qi, 0, True)

    @pl.when(qi % 2 == 1)
    def _():
        scores(qi, 1)
        update(qi - 1, 0, False)
        update(qi, 1, True)

    for hh in heads:
        z = z_ref[0, :, hsl(hh)].astype(F32)
        o = acc_ref[hh] / jnp.sum(l_ref[hh], axis=1, keepdims=True)
        o_ref[0, :, hsl(hh)] = (o * (z * jax.nn.sigmoid(z))).astype(BF16)


def _attn(p, qa, ka):
    b, s, _ = p.shape
    tq = ATT_TQ
    w = ATT_HPS * ATT_HEADDIM
    return pl.pallas_call(
        _attn_kernel,
        grid=(b, ATT_HEADS // ATT_HPS, s // tq),
        in_specs=[
            pl.BlockSpec((1, tq, w), lambda bi, h, i: (bi, i, COL_Q // w + h)),
            pl.BlockSpec((1, tq, LANES), lambda bi, h, i: (bi, i, 0)),
            pl.BlockSpec((1, s, w), lambda bi, h, i: (bi, 0, COL_K // w + h)),
            pl.BlockSpec((1, s, LANES), lambda bi, h, i: (bi, 0, 0)),
            pl.BlockSpec((1, s, w), lambda bi, h, i: (bi, 0, COL_V // w + h)),
            pl.BlockSpec((1, tq, w), lambda bi, h, i: (bi, i, COL_Z_ATT // w + h)),
        ],
        out_specs=pl.BlockSpec((1, tq, w), lambda bi, h, i: (bi, i, h)),
        out_shape=jax.ShapeDtypeStruct((b, s, ATT_INNER), BF16),
        scratch_shapes=[
            pltpu.VMEM((ATT_HPS, s, LANES), BF16),
            pltpu.VMEM((2, ATT_HPS, tq, ATT_TK), F32),
            pltpu.VMEM((2, ATT_HPS, tq, LANES), F32),
            pltpu.VMEM((ATT_HPS, tq, LANES), F32),
            pltpu.VMEM((ATT_HPS, tq, LANES), F32),
            pltpu.VMEM((ATT_HPS, tq, ATT_HEADDIM), F32),
        ],
        compiler_params=pltpu.CompilerParams(
            dimension_semantics=("arbitrary", "arbitrary", "arbitrary"),
            vmem_limit_bytes=VMEM_LIMIT),
        name="fox_attn",
    )(p, qa, p, ka, p, p)


OUT_TM = 512


def _out_kernel(ys_ref, ya_ref, g_ref, x_ref, gate_ref, bg_ref, wps_ref, wpa_ref, wo_ref,
                fnw_ref, o_ref):
    g = jax.nn.sigmoid(g_ref[0].astype(F32) + bg_ref[...])
    ps = _dot(ys_ref[0], wps_ref[...])
    pa = _dot(ya_ref[0], wpa_ref[...])
    merged = g[:, 0:D_MODEL] * ps + g[:, D_MODEL:2 * D_MODEL] * pa
    out = _dot(merged.astype(BF16), wo_ref[...])
    xn = x_ref[0] + gate_ref[0] * out
    ms = jnp.mean(xn * xn, axis=-1, keepdims=True)
    o_ref[0] = xn * lax.rsqrt(ms + EPS) * fnw_ref[...]


def _outproj(y_ssm, y_att, p, x, gate, b_gate, wps, wpa, wo, fnw):
    b, s, d = x.shape
    tm = OUT_TM
    const = lambda shape: pl.BlockSpec(shape, lambda bi, i: tuple(0 for _ in shape))
    return pl.pallas_call(
        _out_kernel,
        grid=(b, s // tm),
        in_specs=[
            pl.BlockSpec((1, tm, d), lambda bi, i: (bi, i, 0)),
            pl.BlockSpec((1, tm, d), lambda bi, i: (bi, i, 0)),
            pl.BlockSpec((1, tm, 2 * d), lambda bi, i: (bi, i, COL_G // (2 * d))),
            pl.BlockSpec((1, tm, d), lambda bi, i: (bi, i, 0)),
            pl.BlockSpec((1, 1, d), lambda bi, i: (bi, 0, 0)),
            const((1, 2 * d)), const((d, d)), const((d, d)), const((d, d)), const((1, d)),
        ],
        out_specs=pl.BlockSpec((1, tm, d), lambda bi, i: (bi, i, 0)),
        out_shape=jax.ShapeDtypeStruct((b, s, d), F32),
        compiler_params=pltpu.CompilerParams(
            dimension_semantics=("arbitrary", "arbitrary"), vmem_limit_bytes=VMEM_LIMIT),
        name="outproj",
    )(y_ssm, y_att, p, x, gate, b_gate, wps, wpa, wo, fnw)


def _constants():
    tril = np.tril(np.ones((CHUNK, CHUNK), np.float32))
    ee = np.zeros((2 * LANES, SSM_INNER), np.float32)
    for h in range(SSM_HEADS):
        ee[h, h * SSM_HEADDIM:(h + 1) * SSM_HEADDIM] = 1.0
        ee[LANES + h, h * SSM_HEADDIM:(h + 1) * SSM_HEADDIM] = 1.0
    eq = np.zeros((3 * LANES, LANES), np.float32)
    ek = np.zeros((3 * LANES, LANES), np.float32)
    for h in range(ATT_HEADS):
        base = h * AUG_W
        for part in range(3):
            eq[part * LANES + F_LANE0 + h, base + 3 + part] = 1.0
            ek[part * LANES + F_LANE0 + h, base + part] = -1.0
            eq[ONE_LANE, base + part] = 1.0
            ek[ONE_LANE, base + 3 + part] = 1.0
    as_bf16 = lambda a: jnp.asarray(a, dtype=BF16)
    return as_bf16(tril), as_bf16(ee), as_bf16(eq), as_bf16(ek)


def kernel(x, c, w_ada, b_ada, norm_w, w_in, conv_w, conv_b, dt_bias, a_log, d_skip,
           ssm_norm_w, b_f, b_gate, w_proj_ssm, w_proj_att, w_out, final_norm_w):
    b, s, d = x.shape
    row = lambda v: v.reshape(1, -1).astype(F32)

    c_pad = jnp.zeros((8, d), F32).at[0:b].set(c)
    ada = _ada(c_pad, w_ada, row(b_ada))[0:b]
    shift = ada[:, 0:d].reshape(b, 1, d)
    scale = ada[:, d:2 * d].reshape(b, 1, d)
    gate = ada[:, 2 * d:3 * d].reshape(b, 1, d)

    o_z, o_xbc, o_dt, o_q, o_k, o_v, o_za, o_f, o_g = np.cumsum(
        [0, SSM_INNER, SSM_INNER + BC_W, SSM_HEADS, ATT_INNER, ATT_INNER, ATT_INNER,
         ATT_INNER, ATT_HEADS]).tolist()
    w_big = jnp.concatenate([
        w_in[:, o_z:o_z + SSM_INNER],
        w_in[:, o_xbc:o_xbc + SSM_INNER],
        w_in[:, o_q:o_q + ATT_INNER],
        w_in[:, o_k:o_k + ATT_INNER],
        w_in[:, o_v:o_v + ATT_INNER],
        w_in[:, o_za:o_za + ATT_INNER],
        w_in[:, o_g:o_g + 2 * d],
        w_in[:, o_xbc + SSM_INNER:o_xbc + SSM_INNER + BC_W],
    ], axis=1).astype(BF16)
    w_sm = jnp.concatenate([
        w_in[:, o_dt:o_dt + SSM_HEADS], w_in[:, o_f:o_f + ATT_HEADS],
        jnp.zeros((d, LANES - SSM_HEADS - ATT_HEADS), F32)], axis=1)
    w_sm_hi = w_sm.astype(BF16)
    w_sm_lo = (w_sm - w_sm_hi.astype(F32)).astype(BF16)
    w_small = jnp.stack([w_sm_hi, w_sm_lo])
    colscale = np.ones((1, P_COLS), np.float32)
    colscale[:, COL_Q:COL_Q + ATT_INNER] = ATT_HEADDIM ** -0.5 * LOG2E
    colscale = jnp.asarray(colscale)

    p, sm = _inproj(x, row(norm_w), scale, shift, w_big, w_small, colscale)

    tril, ee, eq, ek = _constants()
    pad_lanes = lambda v, lane0: jnp.zeros((1, LANES), F32).at[0, lane0:lane0 + v.shape[0]].set(v)
    bias_row = pad_lanes(dt_bias.astype(F32), DT_LANE0) + pad_lanes(b_f.astype(F32), F_LANE0)
    alog_row = pad_lanes(a_log.astype(F32), DT_LANE0)
    dskip_e = jnp.repeat(d_skip.astype(F32), SSM_HEADDIM).reshape(1, SSM_INNER)
    y_ssm, qa, ka = _ssd(
        p, sm,
        conv_w[:, 0:SSM_INNER].astype(F32), row(conv_b[0:SSM_INNER]),
        conv_w[:, SSM_INNER:].astype(F32), row(conv_b[SSM_INNER:]),
        bias_row, alog_row, dskip_e, row(ssm_norm_w), tril, ee, eq, ek)

    y_att = _attn(p, qa, ka)

    return _outproj(y_ssm, y_att, p, x, gate, row(b_gate),
                    w_proj_ssm.astype(BF16), w_proj_att.astype(BF16), w_out.astype(BF16),
                    row(final_norm_w))
```

```python
import functools

import jax
import jax.numpy as jnp
import numpy as np
from jax import lax
from jax.experimental import pallas as pl
from jax.experimental.pallas import tpu as pltpu

F32 = jnp.float32
BF16 = jnp.bfloat16

D_MODEL = 1024
SSM_HEADDIM = 64
SSM_HEADS = 16
SSM_INNER = SSM_HEADS * SSM_HEADDIM
SSM_GROUPS = 2
SSM_STATE = 64
CONV_K = 4
CHUNK = 128
ATT_HEADS = 8
ATT_HEADDIM = 128
ATT_INNER = ATT_HEADS * ATT_HEADDIM
EPS = 1e-6
LOG2E = 1.4426950408889634

LANES = 128
GROUP_W = SSM_INNER // SSM_GROUPS
BC_W = 2 * SSM_GROUPS * SSM_STATE

COL_Z_SSM = 0
COL_XS = 1024
COL_Q = 2048
COL_K = 3072
COL_V = 4096
COL_Z_ATT = 5120
COL_G = 6144
COL_BC = 8192
P_COLS = 8448

DT_LANE0 = 0
F_LANE0 = SSM_HEADS
ONE_LANE = LANES - 1
AUG_W = LANES // ATT_HEADS

VMEM_LIMIT = 56 * 1024 * 1024


def _dot(a, b):
    return jnp.dot(a, b, preferred_element_type=F32)


def _split2(x):
    hi = x.astype(BF16)
    lo = (x - hi.astype(F32)).astype(BF16)
    return hi, lo


def _split3(x):
    hi = x.astype(BF16)
    r1 = x - hi.astype(F32)
    mid = r1.astype(BF16)
    lo = (r1 - mid.astype(F32)).astype(BF16)
    return hi, mid, lo


def _sigmoid(x):
    return 0.5 * jnp.tanh(0.5 * x) + 0.5


def _silu(x):
    h = 0.5 * x
    return h + h * jnp.tanh(h)


def _ada_kernel(c_ref, w_ref, b_ref, o_ref):
    c = c_ref[...]
    cs = c * jax.nn.sigmoid(c)
    w = w_ref[...]
    c_hi, c_lo = _split2(cs)
    w_hi, w_lo = _split2(w)
    o_ref[...] = _dot(c_hi, w_hi) + _dot(c_lo, w_hi) + _dot(c_hi, w_lo) + b_ref[...]


def _ada(c_pad, w_ada, b_ada):
    rows = c_pad.shape[0]
    n = w_ada.shape[1]
    tn = 1024
    return pl.pallas_call(
        _ada_kernel,
        grid=(n // tn,),
        in_specs=[
            pl.BlockSpec((rows, D_MODEL), lambda j: (0, 0)),
            pl.BlockSpec((D_MODEL, tn), lambda j: (0, j)),
            pl.BlockSpec((1, tn), lambda j: (0, j)),
        ],
        out_specs=pl.BlockSpec((rows, tn), lambda j: (0, j)),
        out_shape=jax.ShapeDtypeStruct((rows, n), F32),
        compiler_params=pltpu.CompilerParams(
            dimension_semantics=("arbitrary",), vmem_limit_bytes=VMEM_LIMIT),
        name="ada",
    )(c_pad, w_ada, b_ada)


INPROJ_TM = 1024
INPROJ_TN = 2816
INPROJ_CW = 256


def _inproj_kernel(x_ref, nw_ref, scale_ref, shift_ref, w_ref, ws_ref, cs_ref,
                   p_ref, sm_ref, h_ref):
    @pl.when(pl.program_id(2) == 0)
    def _():
        x = x_ref[0]
        ms = jnp.mean(x * x, axis=-1, keepdims=True)
        y = x * lax.rsqrt(ms + EPS) * nw_ref[...]
        h = y * (1.0 + scale_ref[0]) + shift_ref[0]
        h_hi, h_lo = _split2(h)
        h_ref[...] = h_hi
        sm_ref[0] = _dot(h_hi, ws_ref[0]) + _dot(h_lo, ws_ref[0]) + _dot(h_hi, ws_ref[1])

    h = h_ref[...]
    for c in range(INPROJ_TN // INPROJ_CW):
        sl = slice(c * INPROJ_CW, (c + 1) * INPROJ_CW)
        acc = _dot(h, w_ref[:, sl])
        p_ref[0, :, sl] = (acc * cs_ref[:, sl]).astype(BF16)


def _inproj(x, norm_w, scale, shift, w_big, w_small, colscale):
    b, s, d = x.shape
    tm, tn = INPROJ_TM, INPROJ_TN
    return pl.pallas_call(
        _inproj_kernel,
        grid=(b, s // tm, P_COLS // tn),
        in_specs=[
            pl.BlockSpec((1, tm, d), lambda bi, i, j: (bi, i, 0)),
            pl.BlockSpec((1, d), lambda bi, i, j: (0, 0)),
            pl.BlockSpec((1, 1, d), lambda bi, i, j: (bi, 0, 0)),
            pl.BlockSpec((1, 1, d), lambda bi, i, j: (bi, 0, 0)),
            pl.BlockSpec((d, tn), lambda bi, i, j: (0, j)),
            pl.BlockSpec((2, d, LANES), lambda bi, i, j: (0, 0, 0)),
            pl.BlockSpec((1, tn), lambda bi, i, j: (0, j)),
        ],
        out_specs=[
            pl.BlockSpec((1, tm, tn), lambda bi, i, j: (bi, i, j)),
            pl.BlockSpec((1, tm, LANES), lambda bi, i, j: (bi, i, 0)),
        ],
        out_shape=[
            jax.ShapeDtypeStruct((b, s, P_COLS), BF16),
            jax.ShapeDtypeStruct((b, s, LANES), F32),
        ],
        scratch_shapes=[pltpu.VMEM((tm, d), BF16)],
        compiler_params=pltpu.CompilerParams(
            dimension_semantics=("arbitrary", "arbitrary", "arbitrary"),
            vmem_limit_bytes=VMEM_LIMIT),
        name="inproj",
    )(x, norm_w, scale, shift, w_big, w_small, colscale)


SSD_ROWS = 512
TAIL = 8


def _ssd_kernel(z_ref, xs_ref, bc_ref, sm_ref, cwx_ref, cbx_ref, cwbc_ref, cbbc_ref,
                bias_ref, alog_ref, dskip_ref, nw_ref, tril_ref, ee_ref, eq_ref, ek_ref,
                y_ref, qa_ref, ka_ref,
                extx_ref, extbc_ref, xc_ref, bcc_ref, state_ref, carry_ref):
    rows = SSD_ROWS
    L = CHUNK

    @pl.when(pl.program_id(1) == 0)
    def _():
        extx_ref[:, 0:TAIL, :] = jnp.zeros((SSM_INNER // LANES, TAIL, LANES), F32)
        extbc_ref[:, 0:TAIL, :] = jnp.zeros((BC_W // LANES, TAIL, LANES), F32)
        state_ref[...] = jnp.zeros(state_ref.shape, F32)
        carry_ref[...] = jnp.zeros(carry_ref.shape, F32)

    def conv_silu(src_ref, ext_ref, w_ref, b_ref, dst_ref):
        for c in range(ext_ref.shape[0]):
            csl = slice(c * LANES, (c + 1) * LANES)
            ext_ref[c, TAIL:TAIL + rows, :] = src_ref[0, :, csl].astype(F32)
            acc = b_ref[:, csl] + w_ref[CONV_K - 1:CONV_K, csl] * ext_ref[c, TAIL:TAIL + rows, :]
            for k in range(CONV_K - 1):
                off = TAIL - (CONV_K - 1) + k
                acc = acc + w_ref[k:k + 1, csl] * ext_ref[c, off:off + rows, :]
            dst_ref[:, csl] = _silu(acc)
            ext_ref[c, 0:TAIL, :] = ext_ref[c, rows:rows + TAIL, :]

    conv_silu(xs_ref, extx_ref, cwx_ref, cbx_ref, xc_ref)
    conv_silu(bc_ref, extbc_ref, cwbc_ref, cbbc_ref, bcc_ref)

    lane = lax.broadcasted_iota(jnp.int32, (L, LANES), 1)
    is_dt = lane < SSM_HEADS
    is_f = jnp.logical_and(lane >= F_LANE0, lane < F_LANE0 + ATT_HEADS)
    row_i = lax.broadcasted_iota(jnp.int32, (L, L), 0)
    col_i = lax.broadcasted_iota(jnp.int32, (L, L), 1)
    causal = row_i >= col_i
    neg_a = -jnp.exp(alog_ref[...]) * LOG2E
    tril = tril_ref[...]

    def chunk(c, carry_unused):
        r0 = pl.multiple_of(c * L, L)
        v = sm_ref[0, pl.ds(r0, L), :] + bias_ref[...]
        t = jnp.log1p(jnp.exp(-jnp.abs(v)))
        dt = jnp.maximum(v, 0.0) + t
        lf = jnp.minimum(v, 0.0) - t
        val = jnp.where(is_dt, dt * neg_a, jnp.where(is_f, lf, 0.0))
        v_hi, v_mid, v_lo = _split3(val)
        cs3 = _dot(tril, jnp.concatenate([v_hi, v_mid, v_lo], axis=1))
        cs = cs3[:, 0:LANES] + cs3[:, LANES:2 * LANES] + cs3[:, 2 * LANES:3 * LANES]

        fc = jnp.where(is_f, cs + carry_ref[...], 0.0)
        carry_ref[...] = fc[L - 1:L, :]
        fa = jnp.where(lane == ONE_LANE, 1.0, fc * LOG2E)
        f_hi, f_mid, f_lo = _split3(fa)
        f3 = jnp.concatenate([f_hi, f_mid, f_lo], axis=1)
        qa_ref[0, pl.ds(r0, L), :] = _dot(f3, eq_ref[...]).astype(BF16)
        ka_ref[0, pl.ds(r0, L), :] = _dot(f3, ek_ref[...]).astype(BF16)

        a_cs = jnp.where(is_dt, cs, 0.0)
        a_last = a_cs[L - 1:L, :]
        dtm = jnp.where(is_dt, dt, 0.0)
        w2 = jnp.where(is_dt, jnp.exp2(a_cs), 0.0)
        w1 = dtm * jnp.exp2(a_last - a_cs)
        st_hi, st_lo = _split2(jnp.concatenate([dtm, w2, w1], axis=0))
        ex = _dot(jnp.concatenate([st_hi, st_lo], axis=1), ee_ref[...])
        dt_e = ex[0:L]
        w2_e = ex[L:2 * L]
        w1_e = ex[2 * L:3 * L]

        xsc = xc_ref[pl.ds(r0, L), :]
        bcv = bcc_ref[pl.ds(r0, L), :]
        b_all = bcv[:, 0:LANES]
        c_all = bcv[:, LANES:2 * LANES]
        b_t = b_all.T.astype(BF16)
        xd = xsc * dt_e
        xw1 = (xsc * w1_e).astype(BF16)
        a_cs_t = a_cs.T

        ys = []
        for g in range(SSM_GROUPS):
            gsl = slice(g * GROUP_W, (g + 1) * GROUP_W)
            in_g = jnp.logical_and(lane >= g * SSM_STATE, lane < (g + 1) * SSM_STATE)
            c_m = jnp.where(in_g, c_all, 0.0).astype(BF16)
            cb = _dot(c_m, b_t)
            s_g = state_ref[g]
            y_off = _dot(c_m, s_g.astype(BF16)) * w2_e[:, gsl]
            pieces = []
            for j in range(GROUP_W // LANES):
                h_a = g * (SSM_HEADS // SSM_GROUPS) + 2 * j
                xp = xd[:, g * GROUP_W + j * LANES:g * GROUP_W + (j + 1) * LANES]
                x_a = jnp.where(lane < SSM_HEADDIM, xp, 0.0).astype(BF16)
                x_b = jnp.where(lane >= SSM_HEADDIM, xp, 0.0).astype(BF16)

                def lmat(h):
                    seg = a_cs[:, h:h + 1] - a_cs_t[h:h + 1, :]
                    return (cb * jnp.exp2(jnp.where(causal, seg, -jnp.inf))).astype(BF16)

                pieces.append(_dot(lmat(h_a), x_a) + _dot(lmat(h_a + 1), x_b))
            ys.append(jnp.concatenate(pieces, axis=1) + y_off)
            state_ref[g] = s_g * w2_e[L - 1:L, gsl] + _dot(b_t, xw1[:, gsl])

        y = jnp.concatenate(ys, axis=1) + dskip_ref[...] * xsc
        z = z_ref[0, pl.ds(r0, L), :].astype(F32)
        yg = y * _silu(z)
        ms = jnp.mean(yg * yg, axis=-1, keepdims=True)
        y_ref[0, pl.ds(r0, L), :] = (yg * lax.rsqrt(ms + EPS) * nw_ref[...]).astype(BF16)
        return carry_unused

    lax.fori_loop(0, rows // L, chunk, 0)


def _ssd(p, sm, cwx, cbx, cwbc, cbbc, bias_row, alog_row, dskip_e, nw, tril, ee, eq, ek):
    b, s, _ = p.shape
    rows = SSD_ROWS
    const = lambda shape: pl.BlockSpec(shape, lambda bi, i: tuple(0 for _ in shape))
    out_sds = jax.ShapeDtypeStruct((b, s, SSM_INNER), BF16)
    aug_sds = jax.ShapeDtypeStruct((b, s, LANES), BF16)
    return pl.pallas_call(
        _ssd_kernel,
        grid=(b, s // rows),
        in_specs=[
            pl.BlockSpec((1, rows, SSM_INNER), lambda bi, i: (bi, i, COL_Z_SSM // SSM_INNER)),
            pl.BlockSpec((1, rows, SSM_INNER), lambda bi, i: (bi, i, COL_XS // SSM_INNER)),
            pl.BlockSpec((1, rows, BC_W), lambda bi, i: (bi, i, COL_BC // BC_W)),
            pl.BlockSpec((1, rows, LANES), lambda bi, i: (bi, i, 0)),
            const((CONV_K, SSM_INNER)), const((1, SSM_INNER)),
            const((CONV_K, BC_W)), const((1, BC_W)),
            const((1, LANES)), const((1, LANES)),
            const((1, SSM_INNER)), const((1, SSM_INNER)),
            const((CHUNK, CHUNK)), const((2 * LANES, SSM_INNER)),
            const((3 * LANES, LANES)), const((3 * LANES, LANES)),
        ],
        out_specs=[
            pl.BlockSpec((1, rows, SSM_INNER), lambda bi, i: (bi, i, 0)),
            pl.BlockSpec((1, rows, LANES), lambda bi, i: (bi, i, 0)),
            pl.BlockSpec((1, rows, LANES), lambda bi, i: (bi, i, 0)),
        ],
        out_shape=[out_sds, aug_sds, aug_sds],
        scratch_shapes=[
            pltpu.VMEM((SSM_INNER // LANES, rows + TAIL, LANES), F32),
            pltpu.VMEM((BC_W // LANES, rows + TAIL, LANES), F32),
            pltpu.VMEM((rows, SSM_INNER), F32),
            pltpu.VMEM((rows, BC_W), F32),
            pltpu.VMEM((SSM_GROUPS, LANES, GROUP_W), F32),
            pltpu.VMEM((1, LANES), F32),
        ],
        compiler_params=pltpu.CompilerParams(
            dimension_semantics=("arbitrary", "arbitrary"), vmem_limit_bytes=VMEM_LIMIT),
        name="ssd",
    )(p, p, p, sm, cwx, cbx, cwbc, cbbc, bias_row, alog_row, dskip_e, nw, tril, ee, eq, ek)


ATT_TQ = 512
ATT_TK = 512
ATT_HPS = 2


def _attn_kernel(q_ref, qa_ref, k_ref, ka_ref, v_ref, z_ref, o_ref,
                 kam_ref, s_ref, mx_ref, m_ref, acc_ref):
    tq, tk, hd = ATT_TQ, ATT_TK, ATT_HEADDIM
    qi = pl.program_id(2)
    heads = range(ATT_HPS)
    hsl = lambda hh: slice(hh * hd, (hh + 1) * hd)

    @pl.when(qi == 0)
    def _():
        lane_head = lax.broadcasted_iota(jnp.int32, (tk, LANES), 1) // AUG_W
        for hh in heads:
            own = lane_head == pl.program_id(1) * ATT_HPS + hh

            def fill(t, carry):
                r0 = pl.multiple_of(t * tk, tk)
                blk = ka_ref[0, pl.ds(r0, tk), :].astype(F32)
                kam_ref[hh, pl.ds(r0, tk), :] = jnp.where(own, blk, 0.0).astype(BF16)
                return carry

            lax.fori_loop(0, ka_ref.shape[1] // tk, fill, 0)

    qq = [jnp.concatenate([q_ref[0, :, hsl(hh)], qa_ref[0]], axis=1) for hh in heads]
    m_ref[...] = jnp.full(m_ref.shape, -jnp.inf, F32)
    acc_ref[...] = jnp.zeros(acc_ref.shape, F32)
    ones_v = jnp.ones((tk, hd), BF16)

    def scores(j, slot):
        r0 = pl.multiple_of(j * tk, tk)
        for hh in heads:
            kk = jnp.concatenate(
                [k_ref[0, pl.ds(r0, tk), hsl(hh)], kam_ref[hh, pl.ds(r0, tk), :]], axis=1)
            s = lax.dot_general(qq[hh], kk, (((1,), (1,)), ((), ())), preferred_element_type=F32)
            s_ref[slot, hh] = s
            mx_ref[slot, hh] = jnp.broadcast_to(jnp.max(s, axis=1, keepdims=True), (tq, LANES))

    def update(j, slot, masked):
        r0 = pl.multiple_of(j * tk, tk)
        for hh in heads:
            s = s_ref[slot, hh]
            if masked:
                row = lax.broadcasted_iota(jnp.int32, (tq, tk), 0)
                col = lax.broadcasted_iota(jnp.int32, (tq, tk), 1)
                s = jnp.where(row >= col, s, -jnp.inf)
                mx = jnp.broadcast_to(jnp.max(s, axis=1, keepdims=True), (tq, LANES))
            else:
                mx = mx_ref[slot, hh]
            m_old = m_ref[hh]
            m_new = jnp.maximum(m_old, mx)
            alpha = jnp.exp2(m_old - m_new)
            p = jnp.concatenate(
                [jnp.exp2((s[:, c * LANES:(c + 1) * LANES] - m_new).astype(BF16))
                 for c in range(tk // LANES)], axis=1)
            vv = jnp.concatenate([v_ref[0, pl.ds(r0, tk), hsl(hh)], ones_v], axis=1)
            pv = _dot(p, vv)
            acc_ref[hh, :, 0:hd] = alpha * acc_ref[hh, :, 0:hd] + pv[:, 0:hd]
            acc_ref[hh, :, hd:2 * hd] = alpha * acc_ref[hh, :, hd:2 * hd] + pv[:, hd:2 * hd]
            m_ref[hh] = m_new

    scores(0, 0)

    def body(i, carry):
        j = 2 * i
        scores(j + 1, 1)
        update(j, 0, False)
        scores(j + 2, 0)
        update(j + 1, 1, False)
        return carry

    lax.fori_loop(0, qi // 2, body, 0)

    @pl.when(qi % 2 == 0)
    def _():
        update(qi, 0, True)

    @pl.when(qi % 2 == 1)
    def _():
        scores(qi, 1)
        update(qi - 1, 0, False)
        update(qi, 1, True)

    for hh in heads:
        z = z_ref[0, :, hsl(hh)].astype(F32)
        o = acc_ref[hh, :, 0:hd] / acc_ref[hh, :, hd:2 * hd]
        o_ref[0, :, hsl(hh)] = (o * _silu(z)).astype(BF16)


def _attn(p, qa, ka):
    b, s, _ = p.shape
    tq = ATT_TQ
    w = ATT_HPS * ATT_HEADDIM
    return pl.pallas_call(
        _attn_kernel,
        grid=(b, ATT_HEADS // ATT_HPS, s // tq),
        in_specs=[
            pl.BlockSpec((1, tq, w), lambda bi, h, i: (bi, i, COL_Q // w + h)),
            pl.BlockSpec((1, tq, LANES), lambda bi, h, i: (bi, i, 0)),
            pl.BlockSpec((1, s, w), lambda bi, h, i: (bi, 0, COL_K // w + h)),
            pl.BlockSpec((1, s, LANES), lambda bi, h, i: (bi, 0, 0)),
            pl.BlockSpec((1, s, w), lambda bi, h, i: (bi, 0, COL_V // w + h)),
            pl.BlockSpec((1, tq, w), lambda bi, h, i: (bi, i, COL_Z_ATT // w + h)),
        ],
        out_specs=pl.BlockSpec((1, tq, w), lambda bi, h, i: (bi, i, h)),
        out_shape=jax.ShapeDtypeStruct((b, s, ATT_INNER), BF16),
        scratch_shapes=[
            pltpu.VMEM((ATT_HPS, s, LANES), BF16),
            pltpu.VMEM((2, ATT_HPS, tq, ATT_TK), F32),
            pltpu.VMEM((2, ATT_HPS, tq, LANES), F32),
            pltpu.VMEM((ATT_HPS, tq, LANES), F32),
            pltpu.VMEM((ATT_HPS, tq, 2 * ATT_HEADDIM), F32),
        ],
        compiler_params=pltpu.CompilerParams(
            dimension_semantics=("arbitrary", "arbitrary", "arbitrary"),
            vmem_limit_bytes=VMEM_LIMIT),
        name="fox_attn",
    )(p, qa, p, ka, p, p)


OUT_TM = 512


def _out_kernel(ys_ref, ya_ref, g_ref, x_ref, gate_ref, bg_ref, wps_ref, wpa_ref, wo_ref,
                fnw_ref, o_ref):
    g = _sigmoid(g_ref[0].astype(F32) + bg_ref[...])
    ps = _dot(ys_ref[0], wps_ref[...])
    pa = _dot(ya_ref[0], wpa_ref[...])
    merged = g[:, 0:D_MODEL] * ps + g[:, D_MODEL:2 * D_MODEL] * pa
    out = _dot(merged.astype(BF16), wo_ref[...])
    xn = x_ref[0] + gate_ref[0] * out
    ms = jnp.mean(xn * xn, axis=-1, keepdims=True)
    o_ref[0] = xn * lax.rsqrt(ms + EPS) * fnw_ref[...]


def _outproj(y_ssm, y_att, p, x, gate, b_gate, wps, wpa, wo, fnw):
    b, s, d = x.shape
    tm = OUT_TM
    const = lambda shape: pl.BlockSpec(shape, lambda bi, i: tuple(0 for _ in shape))
    return pl.pallas_call(
        _out_kernel,
        grid=(b, s // tm),
        in_specs=[
            pl.BlockSpec((1, tm, d), lambda bi, i: (bi, i, 0)),
            pl.BlockSpec((1, tm, d), lambda bi, i: (bi, i, 0)),
            pl.BlockSpec((1, tm, 2 * d), lambda bi, i: (bi, i, COL_G // (2 * d))),
            pl.BlockSpec((1, tm, d), lambda bi, i: (bi, i, 0)),
            pl.BlockSpec((1, 1, d), lambda bi, i: (bi, 0, 0)),
            const((1, 2 * d)), const((d, d)), const((d, d)), const((d, d)), const((1, d)),
        ],
        out_specs=pl.BlockSpec((1, tm, d), lambda bi, i: (bi, i, 0)),
        out_shape=jax.ShapeDtypeStruct((b, s, d), F32),
        compiler_params=pltpu.CompilerParams(
            dimension_semantics=("arbitrary", "arbitrary"), vmem_limit_bytes=VMEM_LIMIT),
        name="outproj",
    )(y_ssm, y_att, p, x, gate, b_gate, wps, wpa, wo, fnw)


def _constants():
    tril = np.tril(np.ones((CHUNK, CHUNK), np.float32))
    ee = np.zeros((2 * LANES, SSM_INNER), np.float32)
    for h in range(SSM_HEADS):
        ee[h, h * SSM_HEADDIM:(h + 1) * SSM_HEADDIM] = 1.0
        ee[LANES + h, h * SSM_HEADDIM:(h + 1) * SSM_HEADDIM] = 1.0
    eq = np.zeros((3 * LANES, LANES), np.float32)
    ek = np.zeros((3 * LANES, LANES), np.float32)
    for h in range(ATT_HEADS):
        base = h * AUG_W
        for part in range(3):
            eq[part * LANES + F_LANE0 + h, base + 3 + part] = 1.0
            ek[part * LANES + F_LANE0 + h, base + part] = -1.0
            eq[ONE_LANE, base + part] = 1.0
            ek[ONE_LANE, base + 3 + part] = 1.0
    as_bf16 = lambda a: jnp.asarray(a, dtype=BF16)
    return as_bf16(tril), as_bf16(ee), as_bf16(eq), as_bf16(ek)


def kernel(x, c, w_ada, b_ada, norm_w, w_in, conv_w, conv_b, dt_bias, a_log, d_skip,
           ssm_norm_w, b_f, b_gate, w_proj_ssm, w_proj_att, w_out, final_norm_w):
    b, s, d = x.shape
    row = lambda v: v.reshape(1, -1).astype(F32)

    c_pad = jnp.zeros((8, d), F32).at[0:b].set(c)
    ada = _ada(c_pad, w_ada, row(b_ada))[0:b]
    shift = ada[:, 0:d].reshape(b, 1, d)
    scale = ada[:, d:2 * d].reshape(b, 1, d)
    gate = ada[:, 2 * d:3 * d].reshape(b, 1, d)

    o_z, o_xbc, o_dt, o_q, o_k, o_v, o_za, o_f, o_g = np.cumsum(
        [0, SSM_INNER, SSM_INNER + BC_W, SSM_HEADS, ATT_INNER, ATT_INNER, ATT_INNER,
         ATT_INNER, ATT_HEADS]).tolist()
    w_big = jnp.concatenate([
        w_in[:, o_z:o_z + SSM_INNER],
        w_in[:, o_xbc:o_xbc + SSM_INNER],
        w_in[:, o_q:o_q + ATT_INNER],
        w_in[:, o_k:o_k + ATT_INNER],
        w_in[:, o_v:o_v + ATT_INNER],
        w_in[:, o_za:o_za + ATT_INNER],
        w_in[:, o_g:o_g + 2 * d],
        w_in[:, o_xbc + SSM_INNER:o_xbc + SSM_INNER + BC_W],
    ], axis=1).astype(BF16)
    w_sm = jnp.concatenate([
        w_in[:, o_dt:o_dt + SSM_HEADS], w_in[:, o_f:o_f + ATT_HEADS],
        jnp.zeros((d, LANES - SSM_HEADS - ATT_HEADS), F32)], axis=1)
    w_sm_hi = w_sm.astype(BF16)
    w_sm_lo = (w_sm - w_sm_hi.astype(F32)).astype(BF16)
    w_small = jnp.stack([w_sm_hi, w_sm_lo])
    colscale = np.ones((1, P_COLS), np.float32)
    colscale[:, COL_Q:COL_Q + ATT_INNER] = ATT_HEADDIM ** -0.5 * LOG2E
    colscale = jnp.asarray(colscale)

    p, sm = _inproj(x, row(norm_w), scale, shift, w_big, w_small, colscale)

    tril, ee, eq, ek = _constants()
    pad_lanes = lambda v, lane0: jnp.zeros((1, LANES), F32).at[0, lane0:lane0 + v.shape[0]].set(v)
    bias_row = pad_lanes(dt_bias.astype(F32), DT_LANE0) + pad_lanes(b_f.astype(F32), F_LANE0)
    alog_row = pad_lanes(a_log.astype(F32), DT_LANE0)
    dskip_e = jnp.repeat(d_skip.astype(F32), SSM_HEADDIM).reshape(1, SSM_INNER)
    y_ssm, qa, ka = _ssd(
        p, sm,
        conv_w[:, 0:SSM_INNER].astype(F32), row(conv_b[0:SSM_INNER]),
        conv_w[:, SSM_INNER:].astype(F32), row(conv_b[SSM_INNER:]),
        bias_row, alog_row, dskip_e, row(ssm_norm_w), tril, ee, eq, ek)

    y_att = _attn(p, qa, ka)

    return _outproj(y_ssm, y_att, p, x, gate, row(b_gate),
                    w_proj_ssm.astype(BF16), w_proj_att.astype(BF16), w_out.astype(BF16),
                    row(final_norm_w))
```

```python
import functools

import jax
import jax.numpy as jnp
import numpy as np
from jax import lax
from jax.experimental import pallas as pl
from jax.experimental.pallas import tpu as pltpu

F32 = jnp.float32
BF16 = jnp.bfloat16

D_MODEL = 1024
SSM_HEADDIM = 64
SSM_HEADS = 16
SSM_INNER = SSM_HEADS * SSM_HEADDIM
SSM_GROUPS = 2
SSM_STATE = 64
CONV_K = 4
CHUNK = 128
ATT_HEADS = 8
ATT_HEADDIM = 128
ATT_INNER = ATT_HEADS * ATT_HEADDIM
EPS = 1e-6
LOG2E = 1.4426950408889634

LANES = 128
GROUP_W = SSM_INNER // SSM_GROUPS
BC_W = 2 * SSM_GROUPS * SSM_STATE

COL_Z_SSM = 0
COL_XS = 1024
COL_Q = 2048
COL_K = 3072
COL_V = 4096
COL_Z_ATT = 5120
COL_G = 6144
COL_BC = 8192
P_COLS = 8448

DT_LANE0 = 0
F_LANE0 = SSM_HEADS
ONE_LANE = LANES - 1
AUG_W = LANES // ATT_HEADS

VMEM_LIMIT = 56 * 1024 * 1024


def _dot(a, b):
    return jnp.dot(a, b, preferred_element_type=F32)


def _split2(x):
    hi = x.astype(BF16)
    lo = (x - hi.astype(F32)).astype(BF16)
    return hi, lo


def _split3(x):
    hi = x.astype(BF16)
    r1 = x - hi.astype(F32)
    mid = r1.astype(BF16)
    lo = (r1 - mid.astype(F32)).astype(BF16)
    return hi, mid, lo


def _sigmoid(x):
    return 0.5 * jnp.tanh(0.5 * x) + 0.5


def _silu(x):
    h = 0.5 * x
    return h + h * jnp.tanh(h)


def _ada_kernel(c_ref, w_ref, b_ref, o_ref):
    c = c_ref[...]
    cs = c * jax.nn.sigmoid(c)
    w = w_ref[...]
    c_hi, c_lo = _split2(cs)
    w_hi, w_lo = _split2(w)
    o_ref[...] = _dot(c_hi, w_hi) + _dot(c_lo, w_hi) + _dot(c_hi, w_lo) + b_ref[...]


def _ada(c_pad, w_ada, b_ada):
    rows = c_pad.shape[0]
    n = w_ada.shape[1]
    tn = 1024
    return pl.pallas_call(
        _ada_kernel,
        grid=(n // tn,),
        in_specs=[
            pl.BlockSpec((rows, D_MODEL), lambda j: (0, 0)),
            pl.BlockSpec((D_MODEL, tn), lambda j: (0, j)),
            pl.BlockSpec((1, tn), lambda j: (0, j)),
        ],
        out_specs=pl.BlockSpec((rows, tn), lambda j: (0, j)),
        out_shape=jax.ShapeDtypeStruct((rows, n), F32),
        compiler_params=pltpu.CompilerParams(
            dimension_semantics=("arbitrary",), vmem_limit_bytes=VMEM_LIMIT),
        name="ada",
    )(c_pad, w_ada, b_ada)


INPROJ_TM = 1024
INPROJ_TN = 2816
INPROJ_CW = 256


def _inproj_kernel(x_ref, nw_ref, scale_ref, shift_ref, w_ref, ws_ref, cs_ref,
                   p_ref, sm_ref, h_ref):
    @pl.when(pl.program_id(2) == 0)
    def _():
        x = x_ref[0]
        ms = jnp.mean(x * x, axis=-1, keepdims=True)
        y = x * lax.rsqrt(ms + EPS) * nw_ref[...]
        h = y * (1.0 + scale_ref[0]) + shift_ref[0]
        h_hi, h_lo = _split2(h)
        h_ref[...] = h_hi
        hh = _dot(h_hi, ws_ref[...])
        sm_ref[0] = hh[:, 0:LANES] + hh[:, LANES:2 * LANES] + _dot(h_lo, ws_ref[:, 0:LANES])

    h = h_ref[...]
    for c in range(INPROJ_TN // INPROJ_CW):
        sl = slice(c * INPROJ_CW, (c + 1) * INPROJ_CW)
        acc = _dot(h, w_ref[:, sl])
        p_ref[0, :, sl] = (acc * cs_ref[:, sl]).astype(BF16)


def _inproj(x, norm_w, scale, shift, w_big, w_small, colscale):
    b, s, d = x.shape
    tm, tn = INPROJ_TM, INPROJ_TN
    return pl.pallas_call(
        _inproj_kernel,
        grid=(b, s // tm, P_COLS // tn),
        in_specs=[
            pl.BlockSpec((1, tm, d), lambda bi, i, j: (bi, i, 0)),
            pl.BlockSpec((1, d), lambda bi, i, j: (0, 0)),
            pl.BlockSpec((1, 1, d), lambda bi, i, j: (bi, 0, 0)),
            pl.BlockSpec((1, 1, d), lambda bi, i, j: (bi, 0, 0)),
            pl.BlockSpec((d, tn), lambda bi, i, j: (0, j)),
            pl.BlockSpec((d, 2 * LANES), lambda bi, i, j: (0, 0)),
            pl.BlockSpec((1, tn), lambda bi, i, j: (0, j)),
        ],
        out_specs=[
            pl.BlockSpec((1, tm, tn), lambda bi, i, j: (bi, i, j)),
            pl.BlockSpec((1, tm, LANES), lambda bi, i, j: (bi, i, 0)),
        ],
        out_shape=[
            jax.ShapeDtypeStruct((b, s, P_COLS), BF16),
            jax.ShapeDtypeStruct((b, s, LANES), F32),
        ],
        scratch_shapes=[pltpu.VMEM((tm, d), BF16)],
        compiler_params=pltpu.CompilerParams(
            dimension_semantics=("arbitrary", "arbitrary", "arbitrary"),
            vmem_limit_bytes=VMEM_LIMIT),
        name="inproj",
    )(x, norm_w, scale, shift, w_big, w_small, colscale)


SSD_ROWS = 512
SSD_UNROLL = 2
TAIL = 8


def _ssd_kernel(z_ref, xs_ref, bc_ref, sm_ref, cwx_ref, cbx_ref, cwbc_ref, cbbc_ref,
                bias_ref, alog_ref, dskip_ref, nw_ref, tril_ref, ee_ref, eq_ref, ek_ref,
                y_ref, qa_ref, ka_ref,
                extx_ref, extbc_ref, xc_ref, bcc_ref, state_ref, carry_ref):
    rows = SSD_ROWS
    L = CHUNK

    @pl.when(pl.program_id(1) == 0)
    def _():
        extx_ref[:, 0:TAIL, :] = jnp.zeros((SSM_INNER // LANES, TAIL, LANES), F32)
        extbc_ref[:, 0:TAIL, :] = jnp.zeros((BC_W // LANES, TAIL, LANES), F32)
        state_ref[...] = jnp.zeros(state_ref.shape, F32)
        carry_ref[...] = jnp.zeros(carry_ref.shape, F32)

    def conv_silu(src_ref, ext_ref, w_ref, b_ref, dst_ref):
        for c in range(ext_ref.shape[0]):
            csl = slice(c * LANES, (c + 1) * LANES)
            ext_ref[c, TAIL:TAIL + rows, :] = src_ref[0, :, csl].astype(F32)
            acc = b_ref[:, csl] + w_ref[CONV_K - 1:CONV_K, csl] * ext_ref[c, TAIL:TAIL + rows, :]
            for k in range(CONV_K - 1):
                off = TAIL - (CONV_K - 1) + k
                acc = acc + w_ref[k:k + 1, csl] * ext_ref[c, off:off + rows, :]
            dst_ref[:, csl] = _silu(acc)
            ext_ref[c, 0:TAIL, :] = ext_ref[c, rows:rows + TAIL, :]

    conv_silu(xs_ref, extx_ref, cwx_ref, cbx_ref, xc_ref)
    conv_silu(bc_ref, extbc_ref, cwbc_ref, cbbc_ref, bcc_ref)

    lane = lax.broadcasted_iota(jnp.int32, (L, LANES), 1)
    is_dt = lane < SSM_HEADS
    is_f = jnp.logical_and(lane >= F_LANE0, lane < F_LANE0 + ATT_HEADS)
    row_i = lax.broadcasted_iota(jnp.int32, (L, L), 0)
    col_i = lax.broadcasted_iota(jnp.int32, (L, L), 1)
    causal = row_i >= col_i
    neg_a = -jnp.exp(alog_ref[...]) * LOG2E
    tril = tril_ref[...]

    def chunk(c):
        r0 = pl.multiple_of(c * L, L)
        v = sm_ref[0, pl.ds(r0, L), :] + bias_ref[...]
        t = jnp.log1p(jnp.exp(-jnp.abs(v)))
        dt = jnp.maximum(v, 0.0) + t
        lf = jnp.minimum(v, 0.0) - t
        val = jnp.where(is_dt, dt * neg_a, jnp.where(is_f, lf, 0.0))
        v_hi, v_mid, v_lo = _split3(val)
        cs3 = _dot(tril, jnp.concatenate([v_hi, v_mid, v_lo], axis=1))
        cs = cs3[:, 0:LANES] + cs3[:, LANES:2 * LANES] + cs3[:, 2 * LANES:3 * LANES]

        fc = jnp.where(is_f, cs + carry_ref[...], 0.0)
        carry_ref[...] = fc[L - 1:L, :]
        fa = jnp.where(lane == ONE_LANE, 1.0, fc * LOG2E)
        f_hi, f_mid, f_lo = _split3(fa)
        f3 = jnp.concatenate([f_hi, f_mid, f_lo], axis=1)
        qa_ref[0, pl.ds(r0, L), :] = _dot(f3, eq_ref[...]).astype(BF16)
        ka_ref[0, pl.ds(r0, L), :] = _dot(f3, ek_ref[...]).astype(BF16)

        a_cs = jnp.where(is_dt, cs, 0.0)
        a_last = a_cs[L - 1:L, :]
        dtm = jnp.where(is_dt, dt, 0.0)
        w2 = jnp.where(is_dt, jnp.exp2(a_cs), 0.0)
        w1 = dtm * jnp.exp2(a_last - a_cs)
        st_hi, st_lo = _split2(jnp.concatenate([dtm, w2, w1], axis=0))
        ex = _dot(jnp.concatenate([st_hi, st_lo], axis=1), ee_ref[...])
        dt_e = ex[0:L]
        w2_e = ex[L:2 * L]
        w1_e = ex[2 * L:3 * L]

        xsc = xc_ref[pl.ds(r0, L), :]
        bcv = bcc_ref[pl.ds(r0, L), :]
        b_all = bcv[:, 0:LANES]
        c_all = bcv[:, LANES:2 * LANES]
        b_t = b_all.T.astype(BF16)
        xd = xsc * dt_e
        xw1 = (xsc * w1_e).astype(BF16)
        a_cs_t = a_cs.T

        ys = []
        for g in range(SSM_GROUPS):
            gsl = slice(g * GROUP_W, (g + 1) * GROUP_W)
            in_g = jnp.logical_and(lane >= g * SSM_STATE, lane < (g + 1) * SSM_STATE)
            c_m = jnp.where(in_g, c_all, 0.0).astype(BF16)
            cb = _dot(c_m, b_t)
            s_g = state_ref[g]
            y_off = _dot(c_m, s_g.astype(BF16)) * w2_e[:, gsl]
            pieces = []
            for j in range(GROUP_W // LANES):
                h_a = g * (SSM_HEADS // SSM_GROUPS) + 2 * j
                xp = xd[:, g * GROUP_W + j * LANES:g * GROUP_W + (j + 1) * LANES]
                x_a = jnp.where(lane < SSM_HEADDIM, xp, 0.0).astype(BF16)
                x_b = jnp.where(lane >= SSM_HEADDIM, xp, 0.0).astype(BF16)

                def lmat(h):
                    seg = a_cs[:, h:h + 1] - a_cs_t[h:h + 1, :]
                    return (cb * jnp.exp2(jnp.where(causal, seg, -jnp.inf))).astype(BF16)

                pieces.append(_dot(lmat(h_a), x_a) + _dot(lmat(h_a + 1), x_b))
            ys.append(jnp.concatenate(pieces, axis=1) + y_off)
            state_ref[g] = s_g * w2_e[L - 1:L, gsl] + _dot(b_t, xw1[:, gsl])

        y = jnp.concatenate(ys, axis=1) + dskip_ref[...] * xsc
        z = z_ref[0, pl.ds(r0, L), :].astype(F32)
        yg = y * _silu(z)
        ms = jnp.mean(yg * yg, axis=-1, keepdims=True)
        y_ref[0, pl.ds(r0, L), :] = (yg * lax.rsqrt(ms + EPS) * nw_ref[...]).astype(BF16)

    def chunks(i, carry):
        for u in range(SSD_UNROLL):
            chunk(SSD_UNROLL * i + u)
        return carry

    lax.fori_loop(0, rows // L // SSD_UNROLL, chunks, 0)


def _ssd(p, sm, cwx, cbx, cwbc, cbbc, bias_row, alog_row, dskip_e, nw, tril, ee, eq, ek):
    b, s, _ = p.shape
    rows = SSD_ROWS
    const = lambda shape: pl.BlockSpec(shape, lambda bi, i: tuple(0 for _ in shape))
    out_sds = jax.ShapeDtypeStruct((b, s, SSM_INNER), BF16)
    aug_sds = jax.ShapeDtypeStruct((b, s, LANES), BF16)
    return pl.pallas_call(
        _ssd_kernel,
        grid=(b, s // rows),
        in_specs=[
            pl.BlockSpec((1, rows, SSM_INNER), lambda bi, i: (bi, i, COL_Z_SSM // SSM_INNER)),
            pl.BlockSpec((1, rows, SSM_INNER), lambda bi, i: (bi, i, COL_XS // SSM_INNER)),
            pl.BlockSpec((1, rows, BC_W), lambda bi, i: (bi, i, COL_BC // BC_W)),
            pl.BlockSpec((1, rows, LANES), lambda bi, i: (bi, i, 0)),
            const((CONV_K, SSM_INNER)), const((1, SSM_INNER)),
            const((CONV_K, BC_W)), const((1, BC_W)),
            const((1, LANES)), const((1, LANES)),
            const((1, SSM_INNER)), const((1, SSM_INNER)),
            const((CHUNK, CHUNK)), const((2 * LANES, SSM_INNER)),
            const((3 * LANES, LANES)), const((3 * LANES, LANES)),
        ],
        out_specs=[
            pl.BlockSpec((1, rows, SSM_INNER), lambda bi, i: (bi, i, 0)),
            pl.BlockSpec((1, rows, LANES), lambda bi, i: (bi, i, 0)),
            pl.BlockSpec((1, rows, LANES), lambda bi, i: (bi, i, 0)),
        ],
        out_shape=[out_sds, aug_sds, aug_sds],
        scratch_shapes=[
            pltpu.VMEM((SSM_INNER // LANES, rows + TAIL, LANES), F32),
            pltpu.VMEM((BC_W // LANES, rows + TAIL, LANES), F32),
            pltpu.VMEM((rows, SSM_INNER), F32),
            pltpu.VMEM((rows, BC_W), F32),
            pltpu.VMEM((SSM_GROUPS, LANES, GROUP_W), F32),
            pltpu.VMEM((1, LANES), F32),
        ],
        compiler_params=pltpu.CompilerParams(
            dimension_semantics=("arbitrary", "arbitrary"), vmem_limit_bytes=VMEM_LIMIT),
        name="ssd",
    )(p, p, p, sm, cwx, cbx, cwbc, cbbc, bias_row, alog_row, dskip_e, nw, tril, ee, eq, ek)


ATT_TQ = 512
ATT_TK = 512
ATT_HPS = 2
ATT_UNROLL = 4


def _attn_kernel(q_ref, qa_ref, k_ref, ka_ref, v_ref, z_ref, o_ref,
                 kam_ref, s_ref, mx_ref, m_ref, acc_ref):
    tq, tk, hd = ATT_TQ, ATT_TK, ATT_HEADDIM
    qi = pl.program_id(2)
    heads = range(ATT_HPS)
    hsl = lambda hh: slice(hh * hd, (hh + 1) * hd)

    @pl.when(qi == 0)
    def _():
        lane_head = lax.broadcasted_iota(jnp.int32, (tk, LANES), 1) // AUG_W
        for hh in heads:
            own = lane_head == pl.program_id(1) * ATT_HPS + hh

            def fill(t, carry):
                r0 = pl.multiple_of(t * tk, tk)
                blk = ka_ref[0, pl.ds(r0, tk), :].astype(F32)
                kam_ref[hh, pl.ds(r0, tk), :] = jnp.where(own, blk, 0.0).astype(BF16)
                return carry

            lax.fori_loop(0, ka_ref.shape[1] // tk, fill, 0)

    qq = [jnp.concatenate([q_ref[0, :, hsl(hh)], qa_ref[0]], axis=1) for hh in heads]
    m_ref[...] = jnp.full(m_ref.shape, -jnp.inf, F32)
    acc_ref[...] = jnp.zeros(acc_ref.shape, F32)
    ones_v = jnp.ones((tk, hd), BF16)

    def scores(j, slot):
        r0 = pl.multiple_of(j * tk, tk)
        for hh in heads:
            kk = jnp.concatenate(
                [k_ref[0, pl.ds(r0, tk), hsl(hh)], kam_ref[hh, pl.ds(r0, tk), :]], axis=1)
            s = lax.dot_general(qq[hh], kk, (((1,), (1,)), ((), ())), preferred_element_type=F32)
            s_ref[slot, hh] = s
            mx_ref[slot, hh] = jnp.broadcast_to(jnp.max(s, axis=1, keepdims=True), (tq, LANES))

    def update(j, slot, masked):
        r0 = pl.multiple_of(j * tk, tk)
        for hh in heads:
            s = s_ref[slot, hh]
            if masked:
                row = lax.broadcasted_iota(jnp.int32, (tq, tk), 0)
                col = lax.broadcasted_iota(jnp.int32, (tq, tk), 1)
                s = jnp.where(row >= col, s, -jnp.inf)
                mx = jnp.broadcast_to(jnp.max(s, axis=1, keepdims=True), (tq, LANES))
            else:
                mx = mx_ref[slot, hh]
            m_old = m_ref[hh]
            m_new = jnp.maximum(m_old, mx)
            alpha = jnp.exp2(m_old - m_new)
            p = jnp.concatenate(
                [jnp.exp2((s[:, c * LANES:(c + 1) * LANES] - m_new).astype(BF16))
                 for c in range(tk // LANES)], axis=1)
            vv = jnp.concatenate([v_ref[0, pl.ds(r0, tk), hsl(hh)], ones_v], axis=1)
            pv = _dot(p, vv)
            acc_ref[hh, :, 0:hd] = alpha * acc_ref[hh, :, 0:hd] + pv[:, 0:hd]
            acc_ref[hh, :, hd:2 * hd] = alpha * acc_ref[hh, :, hd:2 * hd] + pv[:, hd:2 * hd]
            m_ref[hh] = m_new

    scores(0, 0)

    def run(j0, n):
        for u in range(n):
            scores(j0 + u + 1, (u + 1) % 2)
            update(j0 + u, u % 2, False)

    def body(i, carry):
        run(ATT_UNROLL * i, ATT_UNROLL)
        return carry

    lax.fori_loop(0, qi // ATT_UNROLL, body, 0)
    for r in range(ATT_UNROLL):
        @pl.when(qi % ATT_UNROLL == r)
        def _(r=r):
            run(qi - r, r)
            update(qi, r % 2, True)

    for hh in heads:
        z = z_ref[0, :, hsl(hh)].astype(F32)
        o = acc_ref[hh, :, 0:hd] / acc_ref[hh, :, hd:2 * hd]
        o_ref[0, :, hsl(hh)] = (o * _silu(z)).astype(BF16)


def _attn(p, qa, ka):
    b, s, _ = p.shape
    tq = ATT_TQ
    w = ATT_HPS * ATT_HEADDIM
    return pl.pallas_call(
        _attn_kernel,
        grid=(b, ATT_HEADS // ATT_HPS, s // tq),
        in_specs=[
            pl.BlockSpec((1, tq, w), lambda bi, h, i: (bi, i, COL_Q // w + h)),
            pl.BlockSpec((1, tq, LANES), lambda bi, h, i: (bi, i, 0)),
            pl.BlockSpec((1, s, w), lambda bi, h, i: (bi, 0, COL_K // w + h)),
            pl.BlockSpec((1, s, LANES), lambda bi, h, i: (bi, 0, 0)),
            pl.BlockSpec((1, s, w), lambda bi, h, i: (bi, 0, COL_V // w + h)),
            pl.BlockSpec((1, tq, w), lambda bi, h, i: (bi, i, COL_Z_ATT // w + h)),
        ],
        out_specs=pl.BlockSpec((1, tq, w), lambda bi, h, i: (bi, i, h)),
        out_shape=jax.ShapeDtypeStruct((b, s, ATT_INNER), BF16),
        scratch_shapes=[
            pltpu.VMEM((ATT_HPS, s, LANES), BF16),
            pltpu.VMEM((2, ATT_HPS, tq, ATT_TK), F32),
            pltpu.VMEM((2, ATT_HPS, tq, LANES), F32),
            pltpu.VMEM((ATT_HPS, tq, LANES), F32),
            pltpu.VMEM((ATT_HPS, tq, 2 * ATT_HEADDIM), F32),
        ],
        compiler_params=pltpu.CompilerParams(
            dimension_semantics=("arbitrary", "arbitrary", "arbitrary"),
            vmem_limit_bytes=VMEM_LIMIT),
        name="fox_attn",
    )(p, qa, p, ka, p, p)


OUT_TM = 512


def _out_kernel(ys_ref, ya_ref, g_ref, x_ref, gate_ref, bg_ref, wps_ref, wpa_ref, wo_ref,
                fnw_ref, o_ref):
    g = _sigmoid(g_ref[0].astype(F32) + bg_ref[...])
    ps = _dot(ys_ref[0], wps_ref[...])
    pa = _dot(ya_ref[0], wpa_ref[...])
    merged = g[:, 0:D_MODEL] * ps + g[:, D_MODEL:2 * D_MODEL] * pa
    out = _dot(merged.astype(BF16), wo_ref[...])
    xn = x_ref[0] + gate_ref[0] * out
    ms = jnp.mean(xn * xn, axis=-1, keepdims=True)
    o_ref[0] = xn * lax.rsqrt(ms + EPS) * fnw_ref[...]


def _outproj(y_ssm, y_att, p, x, gate, b_gate, wps, wpa, wo, fnw):
    b, s, d = x.shape
    tm = OUT_TM
    const = lambda shape: pl.BlockSpec(shape, lambda bi, i: tuple(0 for _ in shape))
    return pl.pallas_call(
        _out_kernel,
        grid=(b, s // tm),
        in_specs=[
            pl.BlockSpec((1, tm, d), lambda bi, i: (bi, i, 0)),
            pl.BlockSpec((1, tm, d), lambda bi, i: (bi, i, 0)),
            pl.BlockSpec((1, tm, 2 * d), lambda bi, i: (bi, i, COL_G // (2 * d))),
            pl.BlockSpec((1, tm, d), lambda bi, i: (bi, i, 0)),
            pl.BlockSpec((1, 1, d), lambda bi, i: (bi, 0, 0)),
            const((1, 2 * d)), const((d, d)), const((d, d)), const((d, d)), const((1, d)),
        ],
        out_specs=pl.BlockSpec((1, tm, d), lambda bi, i: (bi, i, 0)),
        out_shape=jax.ShapeDtypeStruct((b, s, d), F32),
        compiler_params=pltpu.CompilerParams(
            dimension_semantics=("arbitrary", "arbitrary"), vmem_limit_bytes=VMEM_LIMIT),
        name="outproj",
    )(y_ssm, y_att, p, x, gate, b_gate, wps, wpa, wo, fnw)


def _constants():
    tril = np.tril(np.ones((CHUNK, CHUNK), np.float32))
    ee = np.zeros((2 * LANES, SSM_INNER), np.float32)
    for h in range(SSM_HEADS):
        ee[h, h * SSM_HEADDIM:(h + 1) * SSM_HEADDIM] = 1.0
        ee[LANES + h, h * SSM_HEADDIM:(h + 1) * SSM_HEADDIM] = 1.0
    eq = np.zeros((3 * LANES, LANES), np.float32)
    ek = np.zeros((3 * LANES, LANES), np.float32)
    for h in range(ATT_HEADS):
        base = h * AUG_W
        for part in range(3):
            eq[part * LANES + F_LANE0 + h, base + 3 + part] = 1.0
            ek[part * LANES + F_LANE0 + h, base + part] = -1.0
            eq[ONE_LANE, base + part] = 1.0
            ek[ONE_LANE, base + 3 + part] = 1.0
    as_bf16 = lambda a: jnp.asarray(a, dtype=BF16)
    return as_bf16(tril), as_bf16(ee), as_bf16(eq), as_bf16(ek)


def kernel(x, c, w_ada, b_ada, norm_w, w_in, conv_w, conv_b, dt_bias, a_log, d_skip,
           ssm_norm_w, b_f, b_gate, w_proj_ssm, w_proj_att, w_out, final_norm_w):
    b, s, d = x.shape
    row = lambda v: v.reshape(1, -1).astype(F32)

    c_pad = jnp.zeros((8, d), F32).at[0:b].set(c)
    ada = _ada(c_pad, w_ada, row(b_ada))[0:b]
    shift = ada[:, 0:d].reshape(b, 1, d)
    scale = ada[:, d:2 * d].reshape(b, 1, d)
    gate = ada[:, 2 * d:3 * d].reshape(b, 1, d)

    o_z, o_xbc, o_dt, o_q, o_k, o_v, o_za, o_f, o_g = np.cumsum(
        [0, SSM_INNER, SSM_INNER + BC_W, SSM_HEADS, ATT_INNER, ATT_INNER, ATT_INNER,
         ATT_INNER, ATT_HEADS]).tolist()
    w_bf = w_in.astype(BF16)
    w_big = jnp.concatenate([
        w_bf[:, o_z:o_z + SSM_INNER],
        w_bf[:, o_xbc:o_xbc + SSM_INNER],
        w_bf[:, o_q:o_q + ATT_INNER],
        w_bf[:, o_k:o_k + ATT_INNER],
        w_bf[:, o_v:o_v + ATT_INNER],
        w_bf[:, o_za:o_za + ATT_INNER],
        w_bf[:, o_g:o_g + 2 * d],
        w_bf[:, o_xbc + SSM_INNER:o_xbc + SSM_INNER + BC_W],
    ], axis=1)
    w_sm = jnp.concatenate([
        w_in[:, o_dt:o_dt + SSM_HEADS], w_in[:, o_f:o_f + ATT_HEADS],
        jnp.zeros((d, LANES - SSM_HEADS - ATT_HEADS), F32)], axis=1)
    w_sm_hi = w_sm.astype(BF16)
    w_sm_lo = (w_sm - w_sm_hi.astype(F32)).astype(BF16)
    w_small = jnp.concatenate([w_sm_hi, w_sm_lo], axis=1)
    colscale = np.ones((1, P_COLS), np.float32)
    colscale[:, COL_Q:COL_Q + ATT_INNER] = ATT_HEADDIM ** -0.5 * LOG2E
    colscale = jnp.asarray(colscale)

    p, sm = _inproj(x, row(norm_w), scale, shift, w_big, w_small, colscale)

    tril, ee, eq, ek = _constants()
    pad_lanes = lambda v, lane0: jnp.zeros((1, LANES), F32).at[0, lane0:lane0 + v.shape[0]].set(v)
    bias_row = pad_lanes(dt_bias.astype(F32), DT_LANE0) + pad_lanes(b_f.astype(F32), F_LANE0)
    alog_row = pad_lanes(a_log.astype(F32), DT_LANE0)
    dskip_e = jnp.repeat(d_skip.astype(F32), SSM_HEADDIM).reshape(1, SSM_INNER)
    y_ssm, qa, ka = _ssd(
        p, sm,
        conv_w[:, 0:SSM_INNER].astype(F32), row(conv_b[0:SSM_INNER]),
        conv_w[:, SSM_INNER:].astype(F32), row(conv_b[SSM_INNER:]),
        bias_row, alog_row, dskip_e, row(ssm_norm_w), tril, ee, eq, ek)

    y_att = _attn(p, qa, ka)

    return _outproj(y_ssm, y_att, p, x, gate, row(b_gate),
                    w_proj_ssm.astype(BF16), w_proj_att.astype(BF16), w_out.astype(BF16),
                    row(final_norm_w))
```

```python
import functools
import math

import jax
import jax.numpy as jnp
import numpy as np
from jax import lax
from jax.experimental import pallas as pl
from jax.experimental.pallas import tpu as pltpu

F32 = jnp.float32
BF16 = jnp.bfloat16

D_MODEL = 1024
SSM_HEADDIM = 64
SSM_HEADS = 16
SSM_INNER = SSM_HEADS * SSM_HEADDIM
SSM_GROUPS = 2
SSM_STATE = 64
CONV_K = 4
CHUNK = 128
ATT_HEADS = 8
ATT_HEADDIM = 128
ATT_INNER = ATT_HEADS * ATT_HEADDIM
EPS = 1e-6
LOG2E = 1.4426950408889634

LANES = 128
GROUP_W = SSM_INNER // SSM_GROUPS
BC_W = 2 * SSM_GROUPS * SSM_STATE

COL_Z_SSM = 0
COL_XS = 1024
COL_Q = 2048
COL_K = 3072
COL_V = 4096
COL_Z_ATT = 5120
COL_G = 6144
COL_BC = 8192
P_COLS = 8448

DT_LANE0 = 0
F_LANE0 = SSM_HEADS
ONE_LANE = LANES - 1
AUG_W = LANES // ATT_HEADS

VMEM_LIMIT = 56 * 1024 * 1024


def _dot(a, b):
    return jnp.dot(a, b, preferred_element_type=F32)


def _split2(x):
    hi = x.astype(BF16)
    lo = (x - hi.astype(F32)).astype(BF16)
    return hi, lo


def _split3(x):
    hi = x.astype(BF16)
    r1 = x - hi.astype(F32)
    mid = r1.astype(BF16)
    lo = (r1 - mid.astype(F32)).astype(BF16)
    return hi, mid, lo


def _sigmoid(x):
    return 0.5 * jnp.tanh(0.5 * x) + 0.5


def _silu(x):
    h = 0.5 * x
    return h + h * jnp.tanh(h)


def _ada_kernel(c_ref, w_ref, b_ref, o_ref):
    c = c_ref[...]
    cs = c * jax.nn.sigmoid(c)
    w = w_ref[...]
    c_hi, c_lo = _split2(cs)
    w_hi, w_lo = _split2(w)
    o_ref[...] = _dot(c_hi, w_hi) + _dot(c_lo, w_hi) + _dot(c_hi, w_lo) + b_ref[...]


def _ada(c_pad, w_ada, b_ada):
    rows = c_pad.shape[0]
    n = w_ada.shape[1]
    tn = 1024
    return pl.pallas_call(
        _ada_kernel,
        grid=(n // tn,),
        in_specs=[
            pl.BlockSpec((rows, D_MODEL), lambda j: (0, 0)),
            pl.BlockSpec((D_MODEL, tn), lambda j: (0, j)),
            pl.BlockSpec((1, tn), lambda j: (0, j)),
        ],
        out_specs=pl.BlockSpec((rows, tn), lambda j: (0, j)),
        out_shape=jax.ShapeDtypeStruct((rows, n), F32),
        compiler_params=pltpu.CompilerParams(
            dimension_semantics=("arbitrary",), vmem_limit_bytes=VMEM_LIMIT),
        name="ada",
    )(c_pad, w_ada, b_ada)


INPROJ_TM = 1024
INPROJ_TN = 2816
INPROJ_CW = 256


def _inproj_kernel(x_ref, nw_ref, scale_ref, shift_ref, w_ref, ws_ref, cs_ref,
                   p_ref, sm_ref, h_ref):
    @pl.when(pl.program_id(2) == 0)
    def _():
        x = x_ref[0]
        ms = jnp.mean(x * x, axis=-1, keepdims=True)
        y = x * lax.rsqrt(ms + EPS) * nw_ref[...]
        h = y * (1.0 + scale_ref[0]) + shift_ref[0]
        h_hi, h_lo = _split2(h)
        h_ref[...] = h_hi
        hh = _dot(h_hi, ws_ref[...])
        sm_ref[0] = hh[:, 0:LANES] + hh[:, LANES:2 * LANES] + _dot(h_lo, ws_ref[:, 0:LANES])

    h = h_ref[...]
    for c in range(INPROJ_TN // INPROJ_CW):
        sl = slice(c * INPROJ_CW, (c + 1) * INPROJ_CW)
        acc = _dot(h, w_ref[:, sl])
        p_ref[0, :, sl] = (acc * cs_ref[:, sl]).astype(BF16)


def _inproj(x, norm_w, scale, shift, w_big, w_small, colscale):
    b, s, d = x.shape
    tm, tn = INPROJ_TM, INPROJ_TN
    return pl.pallas_call(
        _inproj_kernel,
        grid=(b, s // tm, P_COLS // tn),
        in_specs=[
            pl.BlockSpec((1, tm, d), lambda bi, i, j: (bi, i, 0)),
            pl.BlockSpec((1, d), lambda bi, i, j: (0, 0)),
            pl.BlockSpec((1, 1, d), lambda bi, i, j: (bi, 0, 0)),
            pl.BlockSpec((1, 1, d), lambda bi, i, j: (bi, 0, 0)),
            pl.BlockSpec((d, tn), lambda bi, i, j: (0, j)),
            pl.BlockSpec((d, 2 * LANES), lambda bi, i, j: (0, 0)),
            pl.BlockSpec((1, tn), lambda bi, i, j: (0, j)),
        ],
        out_specs=[
            pl.BlockSpec((1, tm, tn), lambda bi, i, j: (bi, i, j)),
            pl.BlockSpec((1, tm, LANES), lambda bi, i, j: (bi, i, 0)),
        ],
        out_shape=[
            jax.ShapeDtypeStruct((b, s, P_COLS), BF16),
            jax.ShapeDtypeStruct((b, s, LANES), F32),
        ],
        scratch_shapes=[pltpu.VMEM((tm, d), BF16)],
        compiler_params=pltpu.CompilerParams(
            dimension_semantics=("arbitrary", "arbitrary", "arbitrary"),
            vmem_limit_bytes=VMEM_LIMIT),
        name="inproj",
    )(x, norm_w, scale, shift, w_big, w_small, colscale)


SSD_ROWS = 512
SSD_UNROLL = 2
TAIL = 8


def _ssd_kernel(z_ref, xs_ref, bc_ref, sm_ref, cwx_ref, cbx_ref, cwbc_ref, cbbc_ref,
                bias_ref, alog_ref, dskip_ref, nw_ref, tril_ref, ee_ref, eq_ref, ek_ref,
                y_ref, qa_ref, ka_ref,
                extx_ref, extbc_ref, xc_ref, bcc_ref, state_ref, carry_ref):
    rows = SSD_ROWS
    L = CHUNK

    @pl.when(pl.program_id(1) == 0)
    def _():
        extx_ref[:, 0:TAIL, :] = jnp.zeros((SSM_INNER // LANES, TAIL, LANES), F32)
        extbc_ref[:, 0:TAIL, :] = jnp.zeros((BC_W // LANES, TAIL, LANES), F32)
        state_ref[...] = jnp.zeros(state_ref.shape, F32)
        carry_ref[...] = jnp.zeros(carry_ref.shape, F32)

    def conv_silu(src_ref, ext_ref, w_ref, b_ref, dst_ref):
        for c in range(ext_ref.shape[0]):
            csl = slice(c * LANES, (c + 1) * LANES)
            ext_ref[c, TAIL:TAIL + rows, :] = src_ref[0, :, csl].astype(F32)
            acc = b_ref[:, csl] + w_ref[CONV_K - 1:CONV_K, csl] * ext_ref[c, TAIL:TAIL + rows, :]
            for k in range(CONV_K - 1):
                off = TAIL - (CONV_K - 1) + k
                acc = acc + w_ref[k:k + 1, csl] * ext_ref[c, off:off + rows, :]
            dst_ref[:, csl] = _silu(acc)
            ext_ref[c, 0:TAIL, :] = ext_ref[c, rows:rows + TAIL, :]

    conv_silu(xs_ref, extx_ref, cwx_ref, cbx_ref, xc_ref)
    conv_silu(bc_ref, extbc_ref, cwbc_ref, cbbc_ref, bcc_ref)

    lane = lax.broadcasted_iota(jnp.int32, (L, LANES), 1)
    is_dt = lane < SSM_HEADS
    is_f = jnp.logical_and(lane >= F_LANE0, lane < F_LANE0 + ATT_HEADS)
    row_i = lax.broadcasted_iota(jnp.int32, (L, L), 0)
    col_i = lax.broadcasted_iota(jnp.int32, (L, L), 1)
    causal = row_i >= col_i
    neg_a = -jnp.exp(alog_ref[...]) * LOG2E
    tril = tril_ref[...]

    def chunk(c):
        r0 = pl.multiple_of(c * L, L)
        v = sm_ref[0, pl.ds(r0, L), :] + bias_ref[...]
        t = jnp.log(1.0 + jnp.exp(-jnp.abs(v)))
        dt = jnp.maximum(v, 0.0) + t
        lf = jnp.minimum(v, 0.0) - t
        val = jnp.where(is_dt, dt * neg_a, jnp.where(is_f, lf, 0.0))
        v_hi, v_mid, v_lo = _split3(val)
        cs3 = _dot(tril, jnp.concatenate([v_hi, v_mid, v_lo], axis=1))
        cs = cs3[:, 0:LANES] + cs3[:, LANES:2 * LANES] + cs3[:, 2 * LANES:3 * LANES]

        fc = jnp.where(is_f, cs + carry_ref[...], 0.0)
        carry_ref[...] = fc[L - 1:L, :]
        fa = jnp.where(lane == ONE_LANE, 1.0, fc * LOG2E)
        f_hi, f_mid, f_lo = _split3(fa)
        f3 = jnp.concatenate([f_hi, f_mid, f_lo], axis=1)
        qa_ref[0, pl.ds(r0, L), :] = _dot(f3, eq_ref[...]).astype(BF16)
        ka_ref[0, pl.ds(r0, L), :] = _dot(f3, ek_ref[...]).astype(BF16)

        a_cs = jnp.where(is_dt, cs, 0.0)
        a_last = a_cs[L - 1:L, :]
        dtm = jnp.where(is_dt, dt, 0.0)
        w2 = jnp.where(is_dt, jnp.exp2(a_cs), 0.0)
        w1 = dtm * jnp.exp2(a_last - a_cs)
        st_hi, st_lo = _split2(jnp.concatenate([dtm, w2, w1], axis=0))
        ex = _dot(jnp.concatenate([st_hi, st_lo], axis=1), ee_ref[...])
        dt_e = ex[0:L]
        w2_e = ex[L:2 * L]
        w1_e = ex[2 * L:3 * L]

        xsc = xc_ref[pl.ds(r0, L), :]
        bcv = bcc_ref[pl.ds(r0, L), :]
        b_all = bcv[:, 0:LANES]
        c_all = bcv[:, LANES:2 * LANES]
        b_t = b_all.T.astype(BF16)
        xd = xsc * dt_e
        xw1 = (xsc * w1_e).astype(BF16)
        a_cs_t = a_cs.T

        ys = []
        for g in range(SSM_GROUPS):
            gsl = slice(g * GROUP_W, (g + 1) * GROUP_W)
            in_g = jnp.logical_and(lane >= g * SSM_STATE, lane < (g + 1) * SSM_STATE)
            c_m = jnp.where(in_g, c_all, 0.0).astype(BF16)
            cb = _dot(c_m, b_t)
            s_g = state_ref[g]
            y_off = _dot(c_m, s_g.astype(BF16)) * w2_e[:, gsl]
            pieces = []
            for j in range(GROUP_W // LANES):
                h_a = g * (SSM_HEADS // SSM_GROUPS) + 2 * j
                xp = xd[:, g * GROUP_W + j * LANES:g * GROUP_W + (j + 1) * LANES]
                x_a = jnp.where(lane < SSM_HEADDIM, xp, 0.0).astype(BF16)
                x_b = jnp.where(lane >= SSM_HEADDIM, xp, 0.0).astype(BF16)

                def lmat(h):
                    seg = a_cs[:, h:h + 1] - a_cs_t[h:h + 1, :]
                    return (cb * jnp.exp2(jnp.where(causal, seg, -jnp.inf))).astype(BF16)

                pieces.append(_dot(lmat(h_a), x_a) + _dot(lmat(h_a + 1), x_b))
            ys.append(jnp.concatenate(pieces, axis=1) + y_off)
            state_ref[g] = s_g * w2_e[L - 1:L, gsl] + _dot(b_t, xw1[:, gsl])

        y = jnp.concatenate(ys, axis=1) + dskip_ref[...] * xsc
        z = z_ref[0, pl.ds(r0, L), :].astype(F32)
        yg = y * _silu(z)
        ms = jnp.mean(yg * yg, axis=-1, keepdims=True)
        y_ref[0, pl.ds(r0, L), :] = (yg * lax.rsqrt(ms + EPS) * nw_ref[...]).astype(BF16)

    def chunks(i, carry):
        for u in range(SSD_UNROLL):
            chunk(SSD_UNROLL * i + u)
        return carry

    lax.fori_loop(0, rows // L // SSD_UNROLL, chunks, 0)


def _ssd(p, sm, cwx, cbx, cwbc, cbbc, bias_row, alog_row, dskip_e, nw, tril, ee, eq, ek):
    b, s, _ = p.shape
    rows = SSD_ROWS
    const = lambda shape: pl.BlockSpec(shape, lambda bi, i: tuple(0 for _ in shape))
    out_sds = jax.ShapeDtypeStruct((b, s, SSM_INNER), BF16)
    aug_sds = jax.ShapeDtypeStruct((b, s, LANES), BF16)
    return pl.pallas_call(
        _ssd_kernel,
        grid=(b, s // rows),
        in_specs=[
            pl.BlockSpec((1, rows, SSM_INNER), lambda bi, i: (bi, i, COL_Z_SSM // SSM_INNER)),
            pl.BlockSpec((1, rows, SSM_INNER), lambda bi, i: (bi, i, COL_XS // SSM_INNER)),
            pl.BlockSpec((1, rows, BC_W), lambda bi, i: (bi, i, COL_BC // BC_W)),
            pl.BlockSpec((1, rows, LANES), lambda bi, i: (bi, i, 0)),
            const((CONV_K, SSM_INNER)), const((1, SSM_INNER)),
            const((CONV_K, BC_W)), const((1, BC_W)),
            const((1, LANES)), const((1, LANES)),
            const((1, SSM_INNER)), const((1, SSM_INNER)),
            const((CHUNK, CHUNK)), const((2 * LANES, SSM_INNER)),
            const((3 * LANES, LANES)), const((3 * LANES, LANES)),
        ],
        out_specs=[
            pl.BlockSpec((1, rows, SSM_INNER), lambda bi, i: (bi, i, 0)),
            pl.BlockSpec((1, rows, LANES), lambda bi, i: (bi, i, 0)),
            pl.BlockSpec((1, rows, LANES), lambda bi, i: (bi, i, 0)),
        ],
        out_shape=[out_sds, aug_sds, aug_sds],
        scratch_shapes=[
            pltpu.VMEM((SSM_INNER // LANES, rows + TAIL, LANES), F32),
            pltpu.VMEM((BC_W // LANES, rows + TAIL, LANES), F32),
            pltpu.VMEM((rows, SSM_INNER), F32),
            pltpu.VMEM((rows, BC_W), F32),
            pltpu.VMEM((SSM_GROUPS, LANES, GROUP_W), F32),
            pltpu.VMEM((1, LANES), F32),
        ],
        compiler_params=pltpu.CompilerParams(
            dimension_semantics=("arbitrary", "arbitrary"), vmem_limit_bytes=VMEM_LIMIT),
        name="ssd",
    )(p, p, p, sm, cwx, cbx, cwbc, cbbc, bias_row, alog_row, dskip_e, nw, tril, ee, eq, ek)


ATT_TQ = 512
ATT_TK = 512
ATT_HPS = 2
ATT_QSUB = 2
ATT_UNROLL = 4


def _attn_kernel(q_ref, qa_ref, k_ref, ka_ref, v_ref, z_ref, o_ref,
                 kam_ref, s_ref, mx_ref, m_ref, acc_ref):
    tk = ATT_TK

    @pl.when(pl.program_id(2) == 0)
    def _():
        lane_head = lax.broadcasted_iota(jnp.int32, (tk, LANES), 1) // AUG_W
        for hh in range(ATT_HPS):
            own = lane_head == pl.program_id(1) * ATT_HPS + hh

            def fill(t, carry):
                r0 = pl.multiple_of(t * tk, tk)
                blk = ka_ref[0, pl.ds(r0, tk), :].astype(F32)
                kam_ref[hh, pl.ds(r0, tk), :] = jnp.where(own, blk, 0.0).astype(BF16)
                return carry

            lax.fori_loop(0, ka_ref.shape[1] // tk, fill, 0)

    for sub in range(ATT_QSUB):
        _attn_q_tile(pl.program_id(2) * ATT_QSUB + sub, slice(sub * ATT_TQ, (sub + 1) * ATT_TQ),
                     q_ref, qa_ref, k_ref, kam_ref, v_ref, z_ref, o_ref,
                     s_ref, mx_ref, m_ref, acc_ref)


def _attn_q_tile(qi, qrows, q_ref, qa_ref, k_ref, kam_ref, v_ref, z_ref, o_ref,
                 s_ref, mx_ref, m_ref, acc_ref):
    tq, tk, hd = ATT_TQ, ATT_TK, ATT_HEADDIM
    heads = range(ATT_HPS)
    hsl = lambda hh: slice(hh * hd, (hh + 1) * hd)

    qq = [jnp.concatenate([q_ref[0, qrows, hsl(hh)], qa_ref[0, qrows]], axis=1) for hh in heads]
    m_ref[...] = jnp.full(m_ref.shape, -jnp.inf, F32)
    acc_ref[...] = jnp.zeros(acc_ref.shape, F32)
    ones_v = jnp.ones((tk, hd), BF16)

    all_rows = slice(0, tq)

    def scores(j, slot, rsl=all_rows):
        r0 = pl.multiple_of(j * tk, tk)
        nr = rsl.stop - rsl.start
        for hh in heads:
            kk = jnp.concatenate(
                [k_ref[0, pl.ds(r0, tk), hsl(hh)], kam_ref[hh, pl.ds(r0, tk), :]], axis=1)
            s = lax.dot_general(qq[hh][rsl], kk, (((1,), (1,)), ((), ())),
                                preferred_element_type=F32)
            s_ref[slot, hh, rsl] = s
            mx_ref[slot, hh, rsl] = jnp.broadcast_to(
                jnp.max(s, axis=1, keepdims=True), (nr, LANES))

    def update(j, slot, rsl=all_rows, masked=False):
        r0 = pl.multiple_of(j * tk, tk)
        nr = rsl.stop - rsl.start
        for hh in heads:
            s = s_ref[slot, hh, rsl]
            if masked:
                row = lax.broadcasted_iota(jnp.int32, (nr, tk), 0)
                col = lax.broadcasted_iota(jnp.int32, (nr, tk), 1)
                s = jnp.where(row >= col, s, -jnp.inf)
                mx = jnp.broadcast_to(jnp.max(s, axis=1, keepdims=True), (nr, LANES))
            else:
                mx = mx_ref[slot, hh, rsl]
            m_old = m_ref[hh, rsl]
            m_new = jnp.maximum(m_old, mx)
            alpha = jnp.exp2(m_old - m_new)
            p = jnp.concatenate(
                [jnp.exp2((s[:, c * LANES:(c + 1) * LANES] - m_new).astype(BF16))
                 for c in range(tk // LANES)], axis=1)
            vv = jnp.concatenate([v_ref[0, pl.ds(r0, tk), hsl(hh)], ones_v], axis=1)
            pv = _dot(p, vv)
            acc_ref[hh, rsl, 0:hd] = alpha * acc_ref[hh, rsl, 0:hd] + pv[:, 0:hd]
            acc_ref[hh, rsl, hd:2 * hd] = alpha * acc_ref[hh, rsl, hd:2 * hd] + pv[:, hd:2 * hd]
            m_ref[hh, rsl] = m_new

    scores(0, 0)

    def run(j0, n):
        for u in range(n):
            scores(j0 + u + 1, (u + 1) % 2)
            update(j0 + u, u % 2)

    def body(i, carry):
        run(ATT_UNROLL * i, ATT_UNROLL)
        return carry

    def diagonal_band(j0, slot0):
        for d in range(band):
            if d + 1 < band:
                scores(j0 + d + 1, (slot0 + d + 1) % 2, slice((d + 1) * tk, tq))
            update(j0 + d, (slot0 + d) % 2, slice(d * tk, tq), masked=True)

    band = tq // tk
    n_full = qi * band
    lax.fori_loop(0, n_full // ATT_UNROLL, body, 0)
    rem_step = math.gcd(band, ATT_UNROLL)
    for r in range(0, ATT_UNROLL, rem_step):
        @pl.when(n_full % ATT_UNROLL == r)
        def _(r=r):
            run(n_full - r, r)
            diagonal_band(n_full, r % 2)

    for hh in heads:
        z = z_ref[0, qrows, hsl(hh)].astype(F32)
        o = acc_ref[hh, :, 0:hd] / acc_ref[hh, :, hd:2 * hd]
        o_ref[0, qrows, hsl(hh)] = (o * _silu(z)).astype(BF16)


def _attn(p, qa, ka):
    b, s, _ = p.shape
    tq = ATT_TQ * ATT_QSUB
    w = ATT_HPS * ATT_HEADDIM
    return pl.pallas_call(
        _attn_kernel,
        grid=(b, ATT_HEADS // ATT_HPS, s // tq),
        in_specs=[
            pl.BlockSpec((1, tq, w), lambda bi, h, i: (bi, i, COL_Q // w + h)),
            pl.BlockSpec((1, tq, LANES), lambda bi, h, i: (bi, i, 0)),
            pl.BlockSpec((1, s, w), lambda bi, h, i: (bi, 0, COL_K // w + h)),
            pl.BlockSpec((1, s, LANES), lambda bi, h, i: (bi, 0, 0)),
            pl.BlockSpec((1, s, w), lambda bi, h, i: (bi, 0, COL_V // w + h)),
            pl.BlockSpec((1, tq, w), lambda bi, h, i: (bi, i, COL_Z_ATT // w + h)),
        ],
        out_specs=pl.BlockSpec((1, tq, w), lambda bi, h, i: (bi, i, h)),
        out_shape=jax.ShapeDtypeStruct((b, s, ATT_INNER), BF16),
        scratch_shapes=[
            pltpu.VMEM((ATT_HPS, s, LANES), BF16),
            pltpu.VMEM((2, ATT_HPS, ATT_TQ, ATT_TK), F32),
            pltpu.VMEM((2, ATT_HPS, ATT_TQ, LANES), F32),
            pltpu.VMEM((ATT_HPS, ATT_TQ, LANES), F32),
            pltpu.VMEM((ATT_HPS, ATT_TQ, 2 * ATT_HEADDIM), F32),
        ],
        compiler_params=pltpu.CompilerParams(
            dimension_semantics=("arbitrary", "arbitrary", "arbitrary"),
            vmem_limit_bytes=VMEM_LIMIT),
        name="fox_attn",
    )(p, qa, p, ka, p, p)


OUT_TM = 512


def _out_kernel(ys_ref, ya_ref, g_ref, x_ref, gate_ref, bg_ref, wps_ref, wpa_ref, wo_ref,
                fnw_ref, o_ref):
    g = _sigmoid(g_ref[0].astype(F32) + bg_ref[...])
    ps = _dot(ys_ref[0], wps_ref[...])
    pa = _dot(ya_ref[0], wpa_ref[...])
    merged = g[:, 0:D_MODEL] * ps + g[:, D_MODEL:2 * D_MODEL] * pa
    out = _dot(merged.astype(BF16), wo_ref[...])
    xn = x_ref[0] + gate_ref[0] * out
    ms = jnp.mean(xn * xn, axis=-1, keepdims=True)
    o_ref[0] = xn * lax.rsqrt(ms + EPS) * fnw_ref[...]


def _outproj(y_ssm, y_att, p, x, gate, b_gate, wps, wpa, wo, fnw):
    b, s, d = x.shape
    tm = OUT_TM
    const = lambda shape: pl.BlockSpec(shape, lambda bi, i: tuple(0 for _ in shape))
    return pl.pallas_call(
        _out_kernel,
        grid=(b, s // tm),
        in_specs=[
            pl.BlockSpec((1, tm, d), lambda bi, i: (bi, i, 0)),
            pl.BlockSpec((1, tm, d), lambda bi, i: (bi, i, 0)),
            pl.BlockSpec((1, tm, 2 * d), lambda bi, i: (bi, i, COL_G // (2 * d))),
            pl.BlockSpec((1, tm, d), lambda bi, i: (bi, i, 0)),
            pl.BlockSpec((1, 1, d), lambda bi, i: (bi, 0, 0)),
            const((1, 2 * d)), const((d, d)), const((d, d)), const((d, d)), const((1, d)),
        ],
        out_specs=pl.BlockSpec((1, tm, d), lambda bi, i: (bi, i, 0)),
        out_shape=jax.ShapeDtypeStruct((b, s, d), F32),
        compiler_params=pltpu.CompilerParams(
            dimension_semantics=("arbitrary", "arbitrary"), vmem_limit_bytes=VMEM_LIMIT),
        name="outproj",
    )(y_ssm, y_att, p, x, gate, b_gate, wps, wpa, wo, fnw)


def _constants():
    tril = np.tril(np.ones((CHUNK, CHUNK), np.float32))
    ee = np.zeros((2 * LANES, SSM_INNER), np.float32)
    for h in range(SSM_HEADS):
        ee[h, h * SSM_HEADDIM:(h + 1) * SSM_HEADDIM] = 1.0
        ee[LANES + h, h * SSM_HEADDIM:(h + 1) * SSM_HEADDIM] = 1.0
    eq = np.zeros((3 * LANES, LANES), np.float32)
    ek = np.zeros((3 * LANES, LANES), np.float32)
    for h in range(ATT_HEADS):
        base = h * AUG_W
        for part in range(3):
            eq[part * LANES + F_LANE0 + h, base + 3 + part] = 1.0
            ek[part * LANES + F_LANE0 + h, base + part] = -1.0
            eq[ONE_LANE, base + part] = 1.0
            ek[ONE_LANE, base + 3 + part] = 1.0
    as_bf16 = lambda a: jnp.asarray(a, dtype=BF16)
    return as_bf16(tril), as_bf16(ee), as_bf16(eq), as_bf16(ek)


def kernel(x, c, w_ada, b_ada, norm_w, w_in, conv_w, conv_b, dt_bias, a_log, d_skip,
           ssm_norm_w, b_f, b_gate, w_proj_ssm, w_proj_att, w_out, final_norm_w):
    b, s, d = x.shape
    row = lambda v: v.reshape(1, -1).astype(F32)

    c_pad = jnp.zeros((8, d), F32).at[0:b].set(c)
    ada = _ada(c_pad, w_ada, row(b_ada))[0:b]
    shift = ada[:, 0:d].reshape(b, 1, d)
    scale = ada[:, d:2 * d].reshape(b, 1, d)
    gate = ada[:, 2 * d:3 * d].reshape(b, 1, d)

    o_z, o_xbc, o_dt, o_q, o_k, o_v, o_za, o_f, o_g = np.cumsum(
        [0, SSM_INNER, SSM_INNER + BC_W, SSM_HEADS, ATT_INNER, ATT_INNER, ATT_INNER,
         ATT_INNER, ATT_HEADS]).tolist()
    w_bf = w_in.astype(BF16)
    w_big = jnp.concatenate([
        w_bf[:, o_z:o_z + SSM_INNER],
        w_bf[:, o_xbc:o_xbc + SSM_INNER],
        w_bf[:, o_q:o_q + ATT_INNER],
        w_bf[:, o_k:o_k + ATT_INNER],
        w_bf[:, o_v:o_v + ATT_INNER],
        w_bf[:, o_za:o_za + ATT_INNER],
        w_bf[:, o_g:o_g + 2 * d],
        w_bf[:, o_xbc + SSM_INNER:o_xbc + SSM_INNER + BC_W],
    ], axis=1)
    w_sm = jnp.concatenate([
        w_in[:, o_dt:o_dt + SSM_HEADS], w_in[:, o_f:o_f + ATT_HEADS],
        jnp.zeros((d, LANES - SSM_HEADS - ATT_HEADS), F32)], axis=1)
    w_sm_hi = w_sm.astype(BF16)
    w_sm_lo = (w_sm - w_sm_hi.astype(F32)).astype(BF16)
    w_small = jnp.concatenate([w_sm_hi, w_sm_lo], axis=1)
    colscale = np.ones((1, P_COLS), np.float32)
    colscale[:, COL_Q:COL_Q + ATT_INNER] = ATT_HEADDIM ** -0.5 * LOG2E
    colscale = jnp.asarray(colscale)

    p, sm = _inproj(x, row(norm_w), scale, shift, w_big, w_small, colscale)

    tril, ee, eq, ek = _constants()
    pad_lanes = lambda v, lane0: jnp.zeros((1, LANES), F32).at[0, lane0:lane0 + v.shape[0]].set(v)
    bias_row = pad_lanes(dt_bias.astype(F32), DT_LANE0) + pad_lanes(b_f.astype(F32), F_LANE0)
    alog_row = pad_lanes(a_log.astype(F32), DT_LANE0)
    dskip_e = jnp.repeat(d_skip.astype(F32), SSM_HEADDIM).reshape(1, SSM_INNER)
    y_ssm, qa, ka = _ssd(
        p, sm,
        conv_w[:, 0:SSM_INNER].astype(F32), row(conv_b[0:SSM_INNER]),
        conv_w[:, SSM_INNER:].astype(F32), row(conv_b[SSM_INNER:]),
        bias_row, alog_row, dskip_e, row(ssm_norm_w), tril, ee, eq, ek)

    y_att = _attn(p, qa, ka)

    return _outproj(y_ssm, y_att, p, x, gate, row(b_gate),
                    w_proj_ssm.astype(BF16), w_proj_att.astype(BF16), w_out.astype(BF16),
                    row(final_norm_w))
```

```python
import functools
import math

import jax
import jax.numpy as jnp
import numpy as np
from jax import lax
from jax.experimental import pallas as pl
from jax.experimental.pallas import tpu as pltpu

F32 = jnp.float32
BF16 = jnp.bfloat16

D_MODEL = 1024
SSM_HEADDIM = 64
SSM_HEADS = 16
SSM_INNER = SSM_HEADS * SSM_HEADDIM
SSM_GROUPS = 2
SSM_STATE = 64
CONV_K = 4
CHUNK = 128
ATT_HEADS = 8
ATT_HEADDIM = 128
ATT_INNER = ATT_HEADS * ATT_HEADDIM
EPS = 1e-6
LOG2E = 1.4426950408889634

LANES = 128
GROUP_W = SSM_INNER // SSM_GROUPS
BC_W = 2 * SSM_GROUPS * SSM_STATE

COL_Z_SSM = 0
COL_XS = 1024
COL_Q = 2048
COL_K = 3072
COL_V = 4096
COL_Z_ATT = 5120
COL_G = 6144
COL_BC = 8192
P_COLS = 8448

DT_LANE0 = 0
F_LANE0 = SSM_HEADS
ONE_LANE = LANES - 1
AUG_W = LANES // ATT_HEADS

VMEM_LIMIT = 56 * 1024 * 1024


def _dot(a, b):
    return jnp.dot(a, b, preferred_element_type=F32)


def _split2(x):
    hi = x.astype(BF16)
    lo = (x - hi.astype(F32)).astype(BF16)
    return hi, lo


def _split3(x):
    hi = x.astype(BF16)
    r1 = x - hi.astype(F32)
    mid = r1.astype(BF16)
    lo = (r1 - mid.astype(F32)).astype(BF16)
    return hi, mid, lo


def _sigmoid(x):
    return 0.5 * jnp.tanh(0.5 * x) + 0.5


def _silu(x):
    h = 0.5 * x
    return h + h * jnp.tanh(h)


def _ada_kernel(c_ref, w_ref, b_ref, o_ref):
    c = c_ref[...]
    cs = c * jax.nn.sigmoid(c)
    w = w_ref[...]
    c_hi, c_lo = _split2(cs)
    w_hi, w_lo = _split2(w)
    o_ref[...] = _dot(c_hi, w_hi) + _dot(c_lo, w_hi) + _dot(c_hi, w_lo) + b_ref[...]


def _ada(c_pad, w_ada, b_ada):
    rows = c_pad.shape[0]
    n = w_ada.shape[1]
    tn = 1024
    return pl.pallas_call(
        _ada_kernel,
        grid=(n // tn,),
        in_specs=[
            pl.BlockSpec((rows, D_MODEL), lambda j: (0, 0)),
            pl.BlockSpec((D_MODEL, tn), lambda j: (0, j)),
            pl.BlockSpec((1, tn), lambda j: (0, j)),
        ],
        out_specs=pl.BlockSpec((rows, tn), lambda j: (0, j)),
        out_shape=jax.ShapeDtypeStruct((rows, n), F32),
        compiler_params=pltpu.CompilerParams(
            dimension_semantics=("arbitrary",), vmem_limit_bytes=VMEM_LIMIT),
        name="ada",
    )(c_pad, w_ada, b_ada)


INPROJ_TM = 512
INPROJ_CW = 256
TAIL = 8
Q_SCALE = ATT_HEADDIM ** -0.5 * LOG2E


def _inproj_kernel(x_ref, nw_ref, scale_ref, shift_ref, w_ref, ws_ref, cw_ref, cb_ref,
                   p_ref, sm_ref, ext_ref):
    tm = INPROJ_TM

    @pl.when(pl.program_id(1) == 0)
    def _():
        ext_ref[:, 0:TAIL, :] = jnp.zeros((ext_ref.shape[0], TAIL, LANES), F32)

    x = x_ref[0]
    ms = jnp.mean(x * x, axis=-1, keepdims=True)
    y = x * lax.rsqrt(ms + EPS) * nw_ref[...]
    h = y * (1.0 + scale_ref[0]) + shift_ref[0]
    h_hi, h_lo = _split2(h)
    hh = _dot(h_hi, ws_ref[...])
    sm_ref[0] = hh[:, 0:LANES] + hh[:, LANES:2 * LANES] + _dot(h_lo, ws_ref[:, 0:LANES])

    def conv_channel(col):
        if COL_XS <= col < COL_XS + SSM_INNER:
            return col - COL_XS
        if COL_BC <= col < COL_BC + BC_W:
            return SSM_INNER + col - COL_BC
        return None

    for c in range(P_COLS // INPROJ_CW):
        col0 = c * INPROJ_CW
        acc = _dot(h_hi, w_ref[:, col0:col0 + INPROJ_CW])
        if COL_Q <= col0 < COL_Q + ATT_INNER:
            acc = acc * Q_SCALE
        if conv_channel(col0) is None:
            p_ref[0, :, col0:col0 + INPROJ_CW] = acc.astype(BF16)
            continue
        for t in range(INPROJ_CW // LANES):
            ch0 = conv_channel(col0 + t * LANES)
            e = ch0 // LANES
            chs = slice(ch0, ch0 + LANES)
            u = acc[:, t * LANES:(t + 1) * LANES]
            ext_ref[e, TAIL:TAIL + tm, :] = u
            a = cb_ref[:, chs] + cw_ref[CONV_K - 1:CONV_K, chs] * u
            for k in range(CONV_K - 1):
                off = TAIL - (CONV_K - 1) + k
                a = a + cw_ref[k:k + 1, chs] * ext_ref[e, off:off + tm, :]
            p_ref[0, :, col0 + t * LANES:col0 + (t + 1) * LANES] = _silu(a).astype(BF16)
            ext_ref[e, 0:TAIL, :] = ext_ref[e, tm:tm + TAIL, :]


def _inproj(x, norm_w, scale, shift, w_big, w_small, conv_w, conv_b):
    b, s, d = x.shape
    tm = INPROJ_TM
    conv_dim = SSM_INNER + BC_W
    resident = lambda shape: pl.BlockSpec(shape, lambda bi, i: tuple(0 for _ in shape),
                                          pipeline_mode=pl.Buffered(1))
    return pl.pallas_call(
        _inproj_kernel,
        grid=(b, s // tm),
        in_specs=[
            pl.BlockSpec((1, tm, d), lambda bi, i: (bi, i, 0)),
            pl.BlockSpec((1, d), lambda bi, i: (0, 0)),
            pl.BlockSpec((1, 1, d), lambda bi, i: (bi, 0, 0)),
            pl.BlockSpec((1, 1, d), lambda bi, i: (bi, 0, 0)),
            resident((d, P_COLS)),
            resident((d, 2 * LANES)),
            pl.BlockSpec((CONV_K, conv_dim), lambda bi, i: (0, 0)),
            pl.BlockSpec((1, conv_dim), lambda bi, i: (0, 0)),
        ],
        out_specs=[
            pl.BlockSpec((1, tm, P_COLS), lambda bi, i: (bi, i, 0)),
            pl.BlockSpec((1, tm, LANES), lambda bi, i: (bi, i, 0)),
        ],
        out_shape=[
            jax.ShapeDtypeStruct((b, s, P_COLS), BF16),
            jax.ShapeDtypeStruct((b, s, LANES), F32),
        ],
        scratch_shapes=[pltpu.VMEM((conv_dim // LANES, tm + TAIL, LANES), F32)],
        compiler_params=pltpu.CompilerParams(
            dimension_semantics=("arbitrary", "arbitrary"), vmem_limit_bytes=VMEM_LIMIT),
        name="inproj",
    )(x, norm_w, scale, shift, w_big, w_small, conv_w, conv_b)


SSD_ROWS = 512
SSD_UNROLL = 2


def _ssd_kernel(z_ref, xs_ref, bc_ref, sm_ref,
                bias_ref, alog_ref, dskip_ref, nw_ref, tril_ref, ee_ref, eq_ref, ek_ref,
                y_ref, qa_ref, ka_ref,
                state_ref, carry_ref):
    rows = SSD_ROWS
    L = CHUNK

    @pl.when(pl.program_id(1) == 0)
    def _():
        state_ref[...] = jnp.zeros(state_ref.shape, F32)
        carry_ref[...] = jnp.zeros(carry_ref.shape, F32)

    lane = lax.broadcasted_iota(jnp.int32, (L, LANES), 1)
    is_dt = lane < SSM_HEADS
    is_f = jnp.logical_and(lane >= F_LANE0, lane < F_LANE0 + ATT_HEADS)
    row_i = lax.broadcasted_iota(jnp.int32, (L, L), 0)
    col_i = lax.broadcasted_iota(jnp.int32, (L, L), 1)
    causal = row_i >= col_i
    neg_a = -jnp.exp(alog_ref[...]) * LOG2E
    tril = tril_ref[...]

    def chunk(c):
        r0 = pl.multiple_of(c * L, L)
        v = sm_ref[0, pl.ds(r0, L), :] + bias_ref[...]
        t = jnp.log(1.0 + jnp.exp(-jnp.abs(v)))
        dt = jnp.maximum(v, 0.0) + t
        lf = jnp.minimum(v, 0.0) - t
        val = jnp.where(is_dt, dt * neg_a, jnp.where(is_f, lf, 0.0))
        v_hi, v_mid, v_lo = _split3(val)
        cs3 = _dot(tril, jnp.concatenate([v_hi, v_mid, v_lo], axis=1))
        cs = cs3[:, 0:LANES] + cs3[:, LANES:2 * LANES] + cs3[:, 2 * LANES:3 * LANES]

        fc = jnp.where(is_f, cs + carry_ref[...], 0.0)
        carry_ref[...] = fc[L - 1:L, :]
        fa = jnp.where(lane == ONE_LANE, 1.0, fc * LOG2E)
        f_hi, f_mid, f_lo = _split3(fa)
        f3 = jnp.concatenate([f_hi, f_mid, f_lo], axis=1)
        qa_ref[0, pl.ds(r0, L), :] = _dot(f3, eq_ref[...]).astype(BF16)
        ka_ref[0, pl.ds(r0, L), :] = _dot(f3, ek_ref[...]).astype(BF16)

        a_cs = jnp.where(is_dt, cs, 0.0)
        a_last = a_cs[L - 1:L, :]
        dtm = jnp.where(is_dt, dt, 0.0)
        w2 = jnp.where(is_dt, jnp.exp2(a_cs), 0.0)
        w1 = dtm * jnp.exp2(a_last - a_cs)
        st_hi, st_lo = _split2(jnp.concatenate([dtm, w2, w1], axis=0))
        ex = _dot(jnp.concatenate([st_hi, st_lo], axis=1), ee_ref[...])
        dt_e = ex[0:L]
        w2_e = ex[L:2 * L]
        w1_e = ex[2 * L:3 * L]

        xsc = xs_ref[0, pl.ds(r0, L), :].astype(F32)
        bcv = bc_ref[0, pl.ds(r0, L), :].astype(F32)
        b_all = bcv[:, 0:LANES]
        c_all = bcv[:, LANES:2 * LANES]
        b_t = b_all.T.astype(BF16)
        xd = xsc * dt_e
        xw1 = (xsc * w1_e).astype(BF16)
        a_cs_t = a_cs.T

        ys = []
        for g in range(SSM_GROUPS):
            gsl = slice(g * GROUP_W, (g + 1) * GROUP_W)
            in_g = jnp.logical_and(lane >= g * SSM_STATE, lane < (g + 1) * SSM_STATE)
            c_m = jnp.where(in_g, c_all, 0.0).astype(BF16)
            cb = _dot(c_m, b_t)
            s_g = state_ref[g]
            y_off = _dot(c_m, s_g.astype(BF16)) * w2_e[:, gsl]
            pieces = []
            for j in range(GROUP_W // LANES):
                h_a = g * (SSM_HEADS // SSM_GROUPS) + 2 * j
                xp = xd[:, g * GROUP_W + j * LANES:g * GROUP_W + (j + 1) * LANES]
                x_a = jnp.where(lane < SSM_HEADDIM, xp, 0.0).astype(BF16)
                x_b = jnp.where(lane >= SSM_HEADDIM, xp, 0.0).astype(BF16)

                def lmat(h):
                    seg = a_cs[:, h:h + 1] - a_cs_t[h:h + 1, :]
                    return (cb * jnp.exp2(jnp.where(causal, seg, -jnp.inf))).astype(BF16)

                pieces.append(_dot(lmat(h_a), x_a) + _dot(lmat(h_a + 1), x_b))
            ys.append(jnp.concatenate(pieces, axis=1) + y_off)
            state_ref[g] = s_g * w2_e[L - 1:L, gsl] + _dot(b_t, xw1[:, gsl])

        y = jnp.concatenate(ys, axis=1) + dskip_ref[...] * xsc
        z = z_ref[0, pl.ds(r0, L), :].astype(F32)
        yg = y * _silu(z)
        ms = jnp.mean(yg * yg, axis=-1, keepdims=True)
        y_ref[0, pl.ds(r0, L), :] = (yg * lax.rsqrt(ms + EPS) * nw_ref[...]).astype(BF16)

    def chunks(i, carry):
        for u in range(SSD_UNROLL):
            chunk(SSD_UNROLL * i + u)
        return carry

    lax.fori_loop(0, rows // L // SSD_UNROLL, chunks, 0)


def _ssd(p, sm, bias_row, alog_row, dskip_e, nw, tril, ee, eq, ek):
    b, s, _ = p.shape
    rows = SSD_ROWS
    const = lambda shape: pl.BlockSpec(shape, lambda bi, i: tuple(0 for _ in shape))
    out_sds = jax.ShapeDtypeStruct((b, s, SSM_INNER), BF16)
    aug_sds = jax.ShapeDtypeStruct((b, s, LANES), BF16)
    return pl.pallas_call(
        _ssd_kernel,
        grid=(b, s // rows),
        in_specs=[
            pl.BlockSpec((1, rows, SSM_INNER), lambda bi, i: (bi, i, COL_Z_SSM // SSM_INNER)),
            pl.BlockSpec((1, rows, SSM_INNER), lambda bi, i: (bi, i, COL_XS // SSM_INNER)),
            pl.BlockSpec((1, rows, BC_W), lambda bi, i: (bi, i, COL_BC // BC_W)),
            pl.BlockSpec((1, rows, LANES), lambda bi, i: (bi, i, 0)),
            const((1, LANES)), const((1, LANES)),
            const((1, SSM_INNER)), const((1, SSM_INNER)),
            const((CHUNK, CHUNK)), const((2 * LANES, SSM_INNER)),
            const((3 * LANES, LANES)), const((3 * LANES, LANES)),
        ],
        out_specs=[
            pl.BlockSpec((1, rows, SSM_INNER), lambda bi, i: (bi, i, 0)),
            pl.BlockSpec((1, rows, LANES), lambda bi, i: (bi, i, 0)),
            pl.BlockSpec((1, rows, LANES), lambda bi, i: (bi, i, 0)),
        ],
        out_shape=[out_sds, aug_sds, aug_sds],
        scratch_shapes=[
            pltpu.VMEM((SSM_GROUPS, LANES, GROUP_W), F32),
            pltpu.VMEM((1, LANES), F32),
        ],
        compiler_params=pltpu.CompilerParams(
            dimension_semantics=("arbitrary", "arbitrary"), vmem_limit_bytes=VMEM_LIMIT),
        name="ssd",
    )(p, p, p, sm, bias_row, alog_row, dskip_e, nw, tril, ee, eq, ek)


ATT_TQ = 512
ATT_TK = 512
ATT_HPS = 2
ATT_QSUB = 2
ATT_UNROLL = 4


def _attn_kernel(q_ref, qa_ref, k_ref, ka_ref, v_ref, z_ref, o_ref,
                 kam_ref, s_ref, mx_ref, m_ref, acc_ref):
    tk = ATT_TK

    @pl.when(pl.program_id(2) == 0)
    def _():
        lane_head = lax.broadcasted_iota(jnp.int32, (tk, LANES), 1) // AUG_W
        for hh in range(ATT_HPS):
            own = lane_head == pl.program_id(1) * ATT_HPS + hh

            def fill(t, carry):
                r0 = pl.multiple_of(t * tk, tk)
                blk = ka_ref[0, pl.ds(r0, tk), :].astype(F32)
                kam_ref[hh, pl.ds(r0, tk), :] = jnp.where(own, blk, 0.0).astype(BF16)
                return carry

            lax.fori_loop(0, ka_ref.shape[1] // tk, fill, 0)

    for sub in range(ATT_QSUB):
        _attn_q_tile(pl.program_id(2) * ATT_QSUB + sub, slice(sub * ATT_TQ, (sub + 1) * ATT_TQ),
                     q_ref, qa_ref, k_ref, kam_ref, v_ref, z_ref, o_ref,
                     s_ref, mx_ref, m_ref, acc_ref)


def _attn_q_tile(qi, qrows, q_ref, qa_ref, k_ref, kam_ref, v_ref, z_ref, o_ref,
                 s_ref, mx_ref, m_ref, acc_ref):
    tq, tk, hd = ATT_TQ, ATT_TK, ATT_HEADDIM
    heads = range(ATT_HPS)
    hsl = lambda hh: slice(hh * hd, (hh + 1) * hd)

    qq = [jnp.concatenate([q_ref[0, qrows, hsl(hh)], qa_ref[0, qrows]], axis=1) for hh in heads]
    m_ref[...] = jnp.full(m_ref.shape, -jnp.inf, F32)
    acc_ref[...] = jnp.zeros(acc_ref.shape, F32)
    ones_v = jnp.ones((tk, hd), BF16)

    all_rows = slice(0, tq)

    def scores(j, slot, rsl=all_rows):
        r0 = pl.multiple_of(j * tk, tk)
        nr = rsl.stop - rsl.start
        for hh in heads:
            kk = jnp.concatenate(
                [k_ref[0, pl.ds(r0, tk), hsl(hh)], kam_ref[hh, pl.ds(r0, tk), :]], axis=1)
            s = lax.dot_general(qq[hh][rsl], kk, (((1,), (1,)), ((), ())),
                                preferred_element_type=F32)
            s_ref[slot, hh, rsl] = s
            mx_ref[slot, hh, rsl] = jnp.broadcast_to(
                jnp.max(s, axis=1, keepdims=True), (nr, LANES))

    def update(j, slot, rsl=all_rows, masked=False):
        r0 = pl.multiple_of(j * tk, tk)
        nr = rsl.stop - rsl.start
        for hh in heads:
            s = s_ref[slot, hh, rsl]
            if masked:
                row = lax.broadcasted_iota(jnp.int32, (nr, tk), 0)
                col = lax.broadcasted_iota(jnp.int32, (nr, tk), 1)
                s = jnp.where(row >= col, s, -jnp.inf)
                mx = jnp.broadcast_to(jnp.max(s, axis=1, keepdims=True), (nr, LANES))
            else:
                mx = mx_ref[slot, hh, rsl]
            m_old = m_ref[hh, rsl]
            m_new = jnp.maximum(m_old, mx)
            alpha = jnp.exp2(m_old - m_new)
            p = jnp.concatenate(
                [jnp.exp2((s[:, c * LANES:(c + 1) * LANES] - m_new).astype(BF16))
                 for c in range(tk // LANES)], axis=1)
            vv = jnp.concatenate([v_ref[0, pl.ds(r0, tk), hsl(hh)], ones_v], axis=1)
            pv = _dot(p, vv)
            acc_ref[hh, rsl, 0:hd] = alpha * acc_ref[hh, rsl, 0:hd] + pv[:, 0:hd]
            acc_ref[hh, rsl, hd:2 * hd] = alpha * acc_ref[hh, rsl, hd:2 * hd] + pv[:, hd:2 * hd]
            m_ref[hh, rsl] = m_new

    scores(0, 0)

    def run(j0, n):
        for u in range(n):
            scores(j0 + u + 1, (u + 1) % 2)
            update(j0 + u, u % 2)

    def body(i, carry):
        run(ATT_UNROLL * i, ATT_UNROLL)
        return carry

    def diagonal_band(j0, slot0):
        for d in range(band):
            if d + 1 < band:
                scores(j0 + d + 1, (slot0 + d + 1) % 2, slice((d + 1) * tk, tq))
            update(j0 + d, (slot0 + d) % 2, slice(d * tk, tq), masked=True)

    band = tq // tk
    n_full = qi * band
    lax.fori_loop(0, n_full // ATT_UNROLL, body, 0)
    rem_step = math.gcd(band, ATT_UNROLL)
    for r in range(0, ATT_UNROLL, rem_step):
        @pl.when(n_full % ATT_UNROLL == r)
        def _(r=r):
            run(n_full - r, r)
            diagonal_band(n_full, r % 2)

    for hh in heads:
        z = z_ref[0, qrows, hsl(hh)].astype(F32)
        o = acc_ref[hh, :, 0:hd] / acc_ref[hh, :, hd:2 * hd]
        o_ref[0, qrows, hsl(hh)] = (o * _silu(z)).astype(BF16)


def _attn(p, qa, ka):
    b, s, _ = p.shape
    tq = ATT_TQ * ATT_QSUB
    w = ATT_HPS * ATT_HEADDIM
    return pl.pallas_call(
        _attn_kernel,
        grid=(b, ATT_HEADS // ATT_HPS, s // tq),
        in_specs=[
            pl.BlockSpec((1, tq, w), lambda bi, h, i: (bi, i, COL_Q // w + h)),
            pl.BlockSpec((1, tq, LANES), lambda bi, h, i: (bi, i, 0)),
            pl.BlockSpec((1, s, w), lambda bi, h, i: (bi, 0, COL_K // w + h)),
            pl.BlockSpec((1, s, LANES), lambda bi, h, i: (bi, 0, 0)),
            pl.BlockSpec((1, s, w), lambda bi, h, i: (bi, 0, COL_V // w + h)),
            pl.BlockSpec((1, tq, w), lambda bi, h, i: (bi, i, COL_Z_ATT // w + h)),
        ],
        out_specs=pl.BlockSpec((1, tq, w), lambda bi, h, i: (bi, i, h)),
        out_shape=jax.ShapeDtypeStruct((b, s, ATT_INNER), BF16),
        scratch_shapes=[
            pltpu.VMEM((ATT_HPS, s, LANES), BF16),
            pltpu.VMEM((2, ATT_HPS, ATT_TQ, ATT_TK), F32),
            pltpu.VMEM((2, ATT_HPS, ATT_TQ, LANES), F32),
            pltpu.VMEM((ATT_HPS, ATT_TQ, LANES), F32),
            pltpu.VMEM((ATT_HPS, ATT_TQ, 2 * ATT_HEADDIM), F32),
        ],
        compiler_params=pltpu.CompilerParams(
            dimension_semantics=("arbitrary", "arbitrary", "arbitrary"),
            vmem_limit_bytes=VMEM_LIMIT),
        name="fox_attn",
    )(p, qa, p, ka, p, p)


OUT_TM = 512


def _out_kernel(ys_ref, ya_ref, g_ref, x_ref, gate_ref, bg_ref, wps_ref, wpa_ref, wo_ref,
                fnw_ref, o_ref):
    g = _sigmoid(g_ref[0].astype(F32) + bg_ref[...])
    ps = _dot(ys_ref[0], wps_ref[...])
    pa = _dot(ya_ref[0], wpa_ref[...])
    merged = g[:, 0:D_MODEL] * ps + g[:, D_MODEL:2 * D_MODEL] * pa
    out = _dot(merged.astype(BF16), wo_ref[...])
    xn = x_ref[0] + gate_ref[0] * out
    ms = jnp.mean(xn * xn, axis=-1, keepdims=True)
    o_ref[0] = xn * lax.rsqrt(ms + EPS) * fnw_ref[...]


def _outproj(y_ssm, y_att, p, x, gate, b_gate, wps, wpa, wo, fnw):
    b, s, d = x.shape
    tm = OUT_TM
    const = lambda shape: pl.BlockSpec(shape, lambda bi, i: tuple(0 for _ in shape))
    return pl.pallas_call(
        _out_kernel,
        grid=(b, s // tm),
        in_specs=[
            pl.BlockSpec((1, tm, d), lambda bi, i: (bi, i, 0)),
            pl.BlockSpec((1, tm, d), lambda bi, i: (bi, i, 0)),
            pl.BlockSpec((1, tm, 2 * d), lambda bi, i: (bi, i, COL_G // (2 * d))),
            pl.BlockSpec((1, tm, d), lambda bi, i: (bi, i, 0)),
            pl.BlockSpec((1, 1, d), lambda bi, i: (bi, 0, 0)),
            const((1, 2 * d)), const((d, d)), const((d, d)), const((d, d)), const((1, d)),
        ],
        out_specs=pl.BlockSpec((1, tm, d), lambda bi, i: (bi, i, 0)),
        out_shape=jax.ShapeDtypeStruct((b, s, d), F32),
        compiler_params=pltpu.CompilerParams(
            dimension_semantics=("arbitrary", "arbitrary"), vmem_limit_bytes=VMEM_LIMIT),
        name="outproj",
    )(y_ssm, y_att, p, x, gate, b_gate, wps, wpa, wo, fnw)


def _constants():
    tril = np.tril(np.ones((CHUNK, CHUNK), np.float32))
    ee = np.zeros((2 * LANES, SSM_INNER), np.float32)
    for h in range(SSM_HEADS):
        ee[h, h * SSM_HEADDIM:(h + 1) * SSM_HEADDIM] = 1.0
        ee[LANES + h, h * SSM_HEADDIM:(h + 1) * SSM_HEADDIM] = 1.0
    eq = np.zeros((3 * LANES, LANES), np.float32)
    ek = np.zeros((3 * LANES, LANES), np.float32)
    for h in range(ATT_HEADS):
        base = h * AUG_W
        for part in range(3):
            eq[part * LANES + F_LANE0 + h, base + 3 + part] = 1.0
            ek[part * LANES + F_LANE0 + h, base + part] = -1.0
            eq[ONE_LANE, base + part] = 1.0
            ek[ONE_LANE, base + 3 + part] = 1.0
    as_bf16 = lambda a: jnp.asarray(a, dtype=BF16)
    return as_bf16(tril), as_bf16(ee), as_bf16(eq), as_bf16(ek)


def kernel(x, c, w_ada, b_ada, norm_w, w_in, conv_w, conv_b, dt_bias, a_log, d_skip,
           ssm_norm_w, b_f, b_gate, w_proj_ssm, w_proj_att, w_out, final_norm_w):
    b, s, d = x.shape
    row = lambda v: v.reshape(1, -1).astype(F32)

    c_pad = jnp.zeros((8, d), F32).at[0:b].set(c)
    ada = _ada(c_pad, w_ada, row(b_ada))[0:b]
    shift = ada[:, 0:d].reshape(b, 1, d)
    scale = ada[:, d:2 * d].reshape(b, 1, d)
    gate = ada[:, 2 * d:3 * d].reshape(b, 1, d)

    o_z, o_xbc, o_dt, o_q, o_k, o_v, o_za, o_f, o_g = np.cumsum(
        [0, SSM_INNER, SSM_INNER + BC_W, SSM_HEADS, ATT_INNER, ATT_INNER, ATT_INNER,
         ATT_INNER, ATT_HEADS]).tolist()
    w_bf = w_in.astype(BF16)
    w_big = jnp.concatenate([
        w_bf[:, o_z:o_z + SSM_INNER],
        w_bf[:, o_xbc:o_xbc + SSM_INNER],
        w_bf[:, o_q:o_q + ATT_INNER],
        w_bf[:, o_k:o_k + ATT_INNER],
        w_bf[:, o_v:o_v + ATT_INNER],
        w_bf[:, o_za:o_za + ATT_INNER],
        w_bf[:, o_g:o_g + 2 * d],
        w_bf[:, o_xbc + SSM_INNER:o_xbc + SSM_INNER + BC_W],
    ], axis=1)
    w_sm = jnp.concatenate([
        w_in[:, o_dt:o_dt + SSM_HEADS], w_in[:, o_f:o_f + ATT_HEADS],
        jnp.zeros((d, LANES - SSM_HEADS - ATT_HEADS), F32)], axis=1)
    w_sm_hi = w_sm.astype(BF16)
    w_sm_lo = (w_sm - w_sm_hi.astype(F32)).astype(BF16)
    w_small = jnp.concatenate([w_sm_hi, w_sm_lo], axis=1)

    p, sm = _inproj(x, row(norm_w), scale, shift, w_big, w_small,
                    conv_w.astype(F32), row(conv_b))

    tril, ee, eq, ek = _constants()
    pad_lanes = lambda v, lane0: jnp.zeros((1, LANES), F32).at[0, lane0:lane0 + v.shape[0]].set(v)
    bias_row = pad_lanes(dt_bias.astype(F32), DT_LANE0) + pad_lanes(b_f.astype(F32), F_LANE0)
    alog_row = pad_lanes(a_log.astype(F32), DT_LANE0)
    dskip_e = jnp.repeat(d_skip.astype(F32), SSM_HEADDIM).reshape(1, SSM_INNER)
    y_ssm, qa, ka = _ssd(
        p, sm, bias_row, alog_row, dskip_e, row(ssm_norm_w), tril, ee, eq, ek)

    y_att = _attn(p, qa, ka)

    return _outproj(y_ssm, y_att, p, x, gate, row(b_gate),
                    w_proj_ssm.astype(BF16), w_proj_att.astype(BF16), w_out.astype(BF16),
                    row(final_norm_w))
```

```python
import functools

import jax
import jax.numpy as jnp
import numpy as np
from jax import lax
from jax.experimental import pallas as pl
from jax.experimental.pallas import tpu as pltpu

F32 = jnp.float32
BF16 = jnp.bfloat16

D_MODEL = 1024
SSM_HEADDIM = 64
SSM_HEADS = 16
SSM_INNER = SSM_HEADS * SSM_HEADDIM
SSM_GROUPS = 2
SSM_STATE = 64
CONV_K = 4
CHUNK = 128
ATT_HEADS = 8
ATT_HEADDIM = 128
ATT_INNER = ATT_HEADS * ATT_HEADDIM
EPS = 1e-6
LOG2E = 1.4426950408889634

LANES = 128
GROUP_W = SSM_INNER // SSM_GROUPS
BC_W = 2 * SSM_GROUPS * SSM_STATE

COL_Z_SSM = 0
COL_XS = 1024
COL_Q = 2048
COL_K = 3072
COL_V = 4096
COL_Z_ATT = 5120
COL_G = 6144
COL_BC = 8192
P_COLS = 8448

DT_LANE0 = 0
F_LANE0 = SSM_HEADS
ONE_LANE = LANES - 1
AUG_W = LANES // ATT_HEADS

VMEM_LIMIT = 56 * 1024 * 1024


def _dot(a, b):
    return jnp.dot(a, b, preferred_element_type=F32)


def _split2(x):
    hi = x.astype(BF16)
    lo = (x - hi.astype(F32)).astype(BF16)
    return hi, lo


def _split3(x):
    hi = x.astype(BF16)
    r1 = x - hi.astype(F32)
    mid = r1.astype(BF16)
    lo = (r1 - mid.astype(F32)).astype(BF16)
    return hi, mid, lo


def _sigmoid(x):
    return 0.5 * jnp.tanh(0.5 * x) + 0.5


def _silu(x):
    h = 0.5 * x
    return h + h * jnp.tanh(h)


def _ada_kernel(c_ref, w_ref, b_ref, o_ref):
    c = c_ref[...]
    cs = c * jax.nn.sigmoid(c)
    w = w_ref[...]
    c_hi, c_lo = _split2(cs)
    w_hi, w_lo = _split2(w)
    o_ref[...] = _dot(c_hi, w_hi) + _dot(c_lo, w_hi) + _dot(c_hi, w_lo) + b_ref[...]


def _ada(c_pad, w_ada, b_ada):
    rows = c_pad.shape[0]
    n = w_ada.shape[1]
    tn = 1024
    return pl.pallas_call(
        _ada_kernel,
        grid=(n // tn,),
        in_specs=[
            pl.BlockSpec((rows, D_MODEL), lambda j: (0, 0)),
            pl.BlockSpec((D_MODEL, tn), lambda j: (0, j)),
            pl.BlockSpec((1, tn), lambda j: (0, j)),
        ],
        out_specs=pl.BlockSpec((rows, tn), lambda j: (0, j)),
        out_shape=jax.ShapeDtypeStruct((rows, n), F32),
        compiler_params=pltpu.CompilerParams(
            dimension_semantics=("arbitrary",), vmem_limit_bytes=VMEM_LIMIT),
        name="ada",
    )(c_pad, w_ada, b_ada)


INPROJ_TM = 512
INPROJ_CW = 256
TAIL = 8
Q_SCALE = ATT_HEADDIM ** -0.5 * LOG2E


def _inproj_kernel(x_ref, nw_ref, scale_ref, shift_ref, w_ref, ws_ref, cw_ref, cb_ref,
                   p_ref, sm_ref, ext_ref):
    tm = INPROJ_TM

    @pl.when(pl.program_id(1) == 0)
    def _():
        ext_ref[:, 0:TAIL, :] = jnp.zeros((ext_ref.shape[0], TAIL, LANES), F32)

    x = x_ref[0]
    ms = jnp.mean(x * x, axis=-1, keepdims=True)
    y = x * lax.rsqrt(ms + EPS) * nw_ref[...]
    h = y * (1.0 + scale_ref[0]) + shift_ref[0]
    h_hi, h_lo = _split2(h)
    hh = _dot(h_hi, ws_ref[...])
    sm_ref[0] = hh[:, 0:LANES] + hh[:, LANES:2 * LANES] + _dot(h_lo, ws_ref[:, 0:LANES])

    def conv_channel(col):
        if COL_XS <= col < COL_XS + SSM_INNER:
            return col - COL_XS
        if COL_BC <= col < COL_BC + BC_W:
            return SSM_INNER + col - COL_BC
        return None

    for col0 in range(0, P_COLS, INPROJ_CW):
        acc = _dot(h_hi, w_ref[:, col0:col0 + INPROJ_CW])
        if COL_Q <= col0 < COL_Q + ATT_INNER:
            acc = acc * Q_SCALE
        if conv_channel(col0) is None:
            p_ref[0, :, col0:col0 + INPROJ_CW] = acc.astype(BF16)
            continue
        for t in range(INPROJ_CW // LANES):
            ch0 = conv_channel(col0 + t * LANES)
            e = ch0 // LANES
            chs = slice(ch0, ch0 + LANES)
            u = acc[:, t * LANES:(t + 1) * LANES]
            ext_ref[e, TAIL:TAIL + tm, :] = u
            a = cb_ref[:, chs] + cw_ref[CONV_K - 1:CONV_K, chs] * u
            for k in range(CONV_K - 1):
                off = TAIL - (CONV_K - 1) + k
                a = a + cw_ref[k:k + 1, chs] * ext_ref[e, off:off + tm, :]
            p_ref[0, :, col0 + t * LANES:col0 + (t + 1) * LANES] = _silu(a).astype(BF16)
            ext_ref[e, 0:TAIL, :] = ext_ref[e, tm:tm + TAIL, :]


def _inproj(x, norm_w, scale, shift, w_big, w_small, conv_w, conv_b):
    b, s, d = x.shape
    tm = INPROJ_TM
    conv_dim = SSM_INNER + BC_W
    resident = lambda shape: pl.BlockSpec(shape, lambda bi, i: tuple(0 for _ in shape),
                                          pipeline_mode=pl.Buffered(1))
    return pl.pallas_call(
        _inproj_kernel,
        grid=(b, s // tm),
        in_specs=[
            pl.BlockSpec((1, tm, d), lambda bi, i: (bi, i, 0)),
            pl.BlockSpec((1, d), lambda bi, i: (0, 0)),
            pl.BlockSpec((1, 1, d), lambda bi, i: (bi, 0, 0)),
            pl.BlockSpec((1, 1, d), lambda bi, i: (bi, 0, 0)),
            resident((d, P_COLS)),
            resident((d, 2 * LANES)),
            pl.BlockSpec((CONV_K, conv_dim), lambda bi, i: (0, 0)),
            pl.BlockSpec((1, conv_dim), lambda bi, i: (0, 0)),
        ],
        out_specs=[
            pl.BlockSpec((1, tm, P_COLS), lambda bi, i: (bi, i, 0)),
            pl.BlockSpec((1, tm, LANES), lambda bi, i: (bi, i, 0)),
        ],
        out_shape=[
            jax.ShapeDtypeStruct((b, s, P_COLS), BF16),
            jax.ShapeDtypeStruct((b, s, LANES), F32),
        ],
        scratch_shapes=[pltpu.VMEM((conv_dim // LANES, tm + TAIL, LANES), F32)],
        compiler_params=pltpu.CompilerParams(
            dimension_semantics=("arbitrary", "arbitrary"), vmem_limit_bytes=VMEM_LIMIT),
        name="inproj",
    )(x, norm_w, scale, shift, w_big, w_small, conv_w, conv_b)


SSD_ROWS = 512
SSD_UNROLL = 2


def _ssd_kernel(z_ref, xs_ref, bc_ref, sm_ref,
                bias_ref, alog_ref, dskip_ref, nw_ref, tril_ref, ee_ref, eq_ref, ek_ref,
                y_ref, qa_ref, ka_ref,
                state_ref, carry_ref):
    rows = SSD_ROWS
    L = CHUNK

    @pl.when(pl.program_id(1) == 0)
    def _():
        state_ref[...] = jnp.zeros(state_ref.shape, F32)
        carry_ref[...] = jnp.zeros(carry_ref.shape, F32)

    lane = lax.broadcasted_iota(jnp.int32, (L, LANES), 1)
    is_dt = lane < SSM_HEADS
    is_f = jnp.logical_and(lane >= F_LANE0, lane < F_LANE0 + ATT_HEADS)
    row_i = lax.broadcasted_iota(jnp.int32, (L, L), 0)
    col_i = lax.broadcasted_iota(jnp.int32, (L, L), 1)
    causal = row_i >= col_i
    neg_a = -jnp.exp(alog_ref[...]) * LOG2E
    tril = tril_ref[...]

    def chunk(c):
        r0 = pl.multiple_of(c * L, L)
        v = sm_ref[0, pl.ds(r0, L), :] + bias_ref[...]
        t = jnp.log(1.0 + jnp.exp(-jnp.abs(v)))
        dt = jnp.maximum(v, 0.0) + t
        lf = jnp.minimum(v, 0.0) - t
        val = jnp.where(is_dt, dt * neg_a, jnp.where(is_f, lf, 0.0))
        v_hi, v_mid, v_lo = _split3(val)
        cs3 = _dot(tril, jnp.concatenate([v_hi, v_mid, v_lo], axis=1))
        cs = cs3[:, 0:LANES] + cs3[:, LANES:2 * LANES] + cs3[:, 2 * LANES:3 * LANES]

        fc = jnp.where(is_f, cs + carry_ref[...], 0.0)
        carry_ref[...] = fc[L - 1:L, :]
        fa = jnp.where(lane == ONE_LANE, 1.0, fc * LOG2E)
        f_hi, f_mid, f_lo = _split3(fa)
        f3 = jnp.concatenate([f_hi, f_mid, f_lo], axis=1)
        qa_ref[0, pl.ds(r0, L), :] = _dot(f3, eq_ref[...]).astype(BF16)
        ka_ref[0, pl.ds(r0, L), :] = _dot(f3, ek_ref[...]).astype(BF16)

        a_cs = jnp.where(is_dt, cs, 0.0)
        a_last = a_cs[L - 1:L, :]
        dtm = jnp.where(is_dt, dt, 0.0)
        w2 = jnp.where(is_dt, jnp.exp2(a_cs), 0.0)
        w1 = dtm * jnp.exp2(a_last - a_cs)
        st_hi, st_lo = _split2(jnp.concatenate([dtm, w2, w1], axis=0))
        ex = _dot(jnp.concatenate([st_hi, st_lo], axis=1), ee_ref[...])
        dt_e = ex[0:L]
        w2_e = ex[L:2 * L]
        w1_e = ex[2 * L:3 * L]

        xsc = xs_ref[0, pl.ds(r0, L), :].astype(F32)
        bcv = bc_ref[0, pl.ds(r0, L), :].astype(F32)
        b_all = bcv[:, 0:LANES]
        c_all = bcv[:, LANES:2 * LANES]
        b_t = b_all.T.astype(BF16)
        xd = xsc * dt_e
        xw1 = (xsc * w1_e).astype(BF16)
        a_cs_t = a_cs.T

        ys = []
        for g in range(SSM_GROUPS):
            gsl = slice(g * GROUP_W, (g + 1) * GROUP_W)
            in_g = jnp.logical_and(lane >= g * SSM_STATE, lane < (g + 1) * SSM_STATE)
            c_m = jnp.where(in_g, c_all, 0.0).astype(BF16)
            cb = _dot(c_m, b_t)
            s_g = state_ref[g]
            y_off = _dot(c_m, s_g.astype(BF16)) * w2_e[:, gsl]
            pieces = []
            for j in range(GROUP_W // LANES):
                h_a = g * (SSM_HEADS // SSM_GROUPS) + 2 * j
                xp = xd[:, g * GROUP_W + j * LANES:g * GROUP_W + (j + 1) * LANES]
                x_a = jnp.where(lane < SSM_HEADDIM, xp, 0.0).astype(BF16)
                x_b = jnp.where(lane >= SSM_HEADDIM, xp, 0.0).astype(BF16)

                def lmat(h):
                    seg = a_cs[:, h:h + 1] - a_cs_t[h:h + 1, :]
                    return (cb * jnp.exp2(jnp.where(causal, seg, -jnp.inf))).astype(BF16)

                pieces.append(_dot(lmat(h_a), x_a) + _dot(lmat(h_a + 1), x_b))
            ys.append(jnp.concatenate(pieces, axis=1) + y_off)
            state_ref[g] = s_g * w2_e[L - 1:L, gsl] + _dot(b_t, xw1[:, gsl])

        y = jnp.concatenate(ys, axis=1) + dskip_ref[...] * xsc
        z = z_ref[0, pl.ds(r0, L), :].astype(F32)
        yg = y * _silu(z)
        ms = jnp.mean(yg * yg, axis=-1, keepdims=True)
        y_ref[0, pl.ds(r0, L), :] = (yg * lax.rsqrt(ms + EPS) * nw_ref[...]).astype(BF16)

    def chunks(i, carry):
        for u in range(SSD_UNROLL):
            chunk(SSD_UNROLL * i + u)
        return carry

    lax.fori_loop(0, rows // L // SSD_UNROLL, chunks, 0)


def _ssd(p, sm, bias_row, alog_row, dskip_e, nw, tril, ee, eq, ek):
    b, s, _ = p.shape
    rows = SSD_ROWS
    const = lambda shape: pl.BlockSpec(shape, lambda bi, i: tuple(0 for _ in shape))
    out_sds = jax.ShapeDtypeStruct((b, s, SSM_INNER), BF16)
    aug_sds = jax.ShapeDtypeStruct((b, s, LANES), BF16)
    return pl.pallas_call(
        _ssd_kernel,
        grid=(b, s // rows),
        in_specs=[
            pl.BlockSpec((1, rows, SSM_INNER), lambda bi, i: (bi, i, COL_Z_SSM // SSM_INNER)),
            pl.BlockSpec((1, rows, SSM_INNER), lambda bi, i: (bi, i, COL_XS // SSM_INNER)),
            pl.BlockSpec((1, rows, BC_W), lambda bi, i: (bi, i, COL_BC // BC_W)),
            pl.BlockSpec((1, rows, LANES), lambda bi, i: (bi, i, 0)),
            const((1, LANES)), const((1, LANES)),
            const((1, SSM_INNER)), const((1, SSM_INNER)),
            const((CHUNK, CHUNK)), const((2 * LANES, SSM_INNER)),
            const((3 * LANES, LANES)), const((3 * LANES, LANES)),
        ],
        out_specs=[
            pl.BlockSpec((1, rows, SSM_INNER), lambda bi, i: (bi, i, 0)),
            pl.BlockSpec((1, rows, LANES), lambda bi, i: (bi, i, 0)),
            pl.BlockSpec((1, rows, LANES), lambda bi, i: (bi, i, 0)),
        ],
        out_shape=[out_sds, aug_sds, aug_sds],
        scratch_shapes=[
            pltpu.VMEM((SSM_GROUPS, LANES, GROUP_W), F32),
            pltpu.VMEM((1, LANES), F32),
        ],
        compiler_params=pltpu.CompilerParams(
            dimension_semantics=("arbitrary", "arbitrary"), vmem_limit_bytes=VMEM_LIMIT),
        name="ssd",
    )(p, p, p, sm, bias_row, alog_row, dskip_e, nw, tril, ee, eq, ek)


ATT_TQ = 512
ATT_TK = ATT_TQ
ATT_SPARE_SLOT = 2
ATT_HPS = 2
ATT_QSUB = 8
ATT_UNROLL = 4


def _attn_kernel(q_ref, qa_ref, k_ref, ka_ref, v_ref, z_ref, o_ref,
                 kam_ref, s_ref, mx_ref, m_ref, acc_ref):
    tk = ATT_TK

    @pl.when(pl.program_id(2) == 0)
    def _():
        lane_head = lax.broadcasted_iota(jnp.int32, (tk, LANES), 1) // AUG_W
        for hh in range(ATT_HPS):
            own = lane_head == pl.program_id(1) * ATT_HPS + hh

            def fill(t, carry):
                r0 = pl.multiple_of(t * tk, tk)
                blk = ka_ref[0, pl.ds(r0, tk), :].astype(F32)
                kam_ref[hh, pl.ds(r0, tk), :] = jnp.where(own, blk, 0.0).astype(BF16)
                return carry

            lax.fori_loop(0, ka_ref.shape[1] // tk, fill, 0)

    refs = (q_ref, qa_ref, k_ref, kam_ref, v_ref, z_ref, o_ref, s_ref, mx_ref, m_ref, acc_ref)
    _attn_first_scores(_attn_q_operand(0, q_ref, qa_ref), k_ref, kam_ref, s_ref, mx_ref)

    def sub_tile(sub, carry):
        _attn_q_tile(sub, *refs)
        return carry

    lax.fori_loop(0, ATT_QSUB, sub_tile, 0)


def _attn_q_operand(sub, q_ref, qa_ref):
    hd = ATT_HEADDIM
    rows = pl.ds(pl.multiple_of(sub * ATT_TQ, ATT_TQ), ATT_TQ)
    qa = qa_ref[0, rows, :]
    return [jnp.concatenate([q_ref[0, rows, hh * hd:(hh + 1) * hd], qa], axis=1)
            for hh in range(ATT_HPS)]


def _attn_first_scores(qq, k_ref, kam_ref, s_ref, mx_ref):
    hd = ATT_HEADDIM
    for hh in range(ATT_HPS):
        kk = jnp.concatenate([k_ref[0, 0:ATT_TK, hh * hd:(hh + 1) * hd], kam_ref[hh, 0:ATT_TK, :]],
                             axis=1)
        s = lax.dot_general(qq[hh], kk, (((1,), (1,)), ((), ())), preferred_element_type=F32)
        s_ref[0, hh] = s
        mx_ref[0, hh] = jnp.broadcast_to(jnp.max(s, axis=1, keepdims=True), (ATT_TQ, LANES))


def _attn_q_tile(sub, q_ref, qa_ref, k_ref, kam_ref, v_ref, z_ref, o_ref,
                 s_ref, mx_ref, m_ref, acc_ref):
    tq, tk, hd = ATT_TQ, ATT_TK, ATT_HEADDIM
    heads = range(ATT_HPS)
    hsl = lambda hh: slice(hh * hd, (hh + 1) * hd)
    qi = pl.program_id(2) * ATT_QSUB + sub
    qrows = pl.ds(pl.multiple_of(sub * tq, tq), tq)

    qq = _attn_q_operand(sub, q_ref, qa_ref)
    m_ref[...] = jnp.full(m_ref.shape, -jnp.inf, F32)
    acc_ref[...] = jnp.zeros(acc_ref.shape, F32)
    ones_v = jnp.ones((tk, hd), BF16)

    def scores(j, slot):
        r0 = pl.multiple_of(j * tk, tk)
        for hh in heads:
            kk = jnp.concatenate(
                [k_ref[0, pl.ds(r0, tk), hsl(hh)], kam_ref[hh, pl.ds(r0, tk), :]], axis=1)
            s = lax.dot_general(qq[hh], kk, (((1,), (1,)), ((), ())), preferred_element_type=F32)
            s_ref[slot, hh] = s
            mx_ref[slot, hh] = jnp.broadcast_to(jnp.max(s, axis=1, keepdims=True), (tq, LANES))

    def update(j, slot, masked=False):
        r0 = pl.multiple_of(j * tk, tk)
        for hh in heads:
            s = s_ref[slot, hh]
            if masked:
                row = lax.broadcasted_iota(jnp.int32, (tq, tk), 0)
                col = lax.broadcasted_iota(jnp.int32, (tq, tk), 1)
                s = jnp.where(row >= col, s, -jnp.inf)
                mx = jnp.broadcast_to(jnp.max(s, axis=1, keepdims=True), (tq, LANES))
            else:
                mx = mx_ref[slot, hh]
            m_old = m_ref[hh]
            m_new = jnp.maximum(m_old, mx)
            alpha = jnp.exp2(m_old - m_new)
            p = jnp.concatenate(
                [jnp.exp2((s[:, c * LANES:(c + 1) * LANES] - m_new).astype(BF16))
                 for c in range(tk // LANES)], axis=1)
            vv = jnp.concatenate([v_ref[0, pl.ds(r0, tk), hsl(hh)], ones_v], axis=1)
            pv = _dot(p, vv)
            acc_ref[hh, :, 0:hd] = alpha * acc_ref[hh, :, 0:hd] + pv[:, 0:hd]
            acc_ref[hh, :, hd:2 * hd] = alpha * acc_ref[hh, :, hd:2 * hd] + pv[:, hd:2 * hd]
            m_ref[hh] = m_new

    def run(j0, n, last_slot=None):
        for u in range(n):
            last = u == n - 1 and last_slot is not None
            scores(j0 + u + 1, last_slot if last else (u + 1) % 2)
            update(j0 + u, u % 2)

    def body(i, carry):
        run(ATT_UNROLL * i, ATT_UNROLL)
        return carry

    def next_first_scores():
        nxt = jnp.minimum(sub + 1, ATT_QSUB - 1)
        _attn_first_scores(_attn_q_operand(nxt, q_ref, qa_ref), k_ref, kam_ref, s_ref, mx_ref)

    lax.fori_loop(0, qi // ATT_UNROLL, body, 0)
    for r in range(ATT_UNROLL):
        @pl.when(qi % ATT_UNROLL == r)
        def _(r=r):
            if r == 0:
                update(qi, 0, masked=True)
                next_first_scores()
            else:
                run(qi - r, r, last_slot=ATT_SPARE_SLOT)
                next_first_scores()
                update(qi, ATT_SPARE_SLOT, masked=True)

    for hh in heads:
        z = z_ref[0, qrows, hsl(hh)].astype(F32)
        o = acc_ref[hh, :, 0:hd] / acc_ref[hh, :, hd:2 * hd]
        o_ref[0, qrows, hsl(hh)] = (o * _silu(z)).astype(BF16)


def _attn(p, qa, ka):
    b, s, _ = p.shape
    tq = ATT_TQ * ATT_QSUB
    w = ATT_HPS * ATT_HEADDIM
    return pl.pallas_call(
        _attn_kernel,
        grid=(b, ATT_HEADS // ATT_HPS, s // tq),
        in_specs=[
            pl.BlockSpec((1, tq, w), lambda bi, h, i: (bi, i, COL_Q // w + h)),
            pl.BlockSpec((1, tq, LANES), lambda bi, h, i: (bi, i, 0)),
            pl.BlockSpec((1, s, w), lambda bi, h, i: (bi, 0, COL_K // w + h)),
            pl.BlockSpec((1, s, LANES), lambda bi, h, i: (bi, 0, 0)),
            pl.BlockSpec((1, s, w), lambda bi, h, i: (bi, 0, COL_V // w + h)),
            pl.BlockSpec((1, tq, w), lambda bi, h, i: (bi, i, COL_Z_ATT // w + h)),
        ],
        out_specs=pl.BlockSpec((1, tq, w), lambda bi, h, i: (bi, i, h)),
        out_shape=jax.ShapeDtypeStruct((b, s, ATT_INNER), BF16),
        scratch_shapes=[
            pltpu.VMEM((ATT_HPS, s, LANES), BF16),
            pltpu.VMEM((3, ATT_HPS, ATT_TQ, ATT_TK), F32),
            pltpu.VMEM((3, ATT_HPS, ATT_TQ, LANES), F32),
            pltpu.VMEM((ATT_HPS, ATT_TQ, LANES), F32),
            pltpu.VMEM((ATT_HPS, ATT_TQ, 2 * ATT_HEADDIM), F32),
        ],
        compiler_params=pltpu.CompilerParams(
            dimension_semantics=("arbitrary", "arbitrary", "arbitrary"),
            vmem_limit_bytes=VMEM_LIMIT),
        name="fox_attn",
    )(p, qa, p, ka, p, p)


OUT_TM = 1024


def _out_kernel(ys_ref, ya_ref, g_ref, x_ref, gate_ref, bg_ref, wps_ref, wpa_ref, wo_ref,
                fnw_ref, o_ref):
    g = _sigmoid(g_ref[0].astype(F32) + bg_ref[...])
    ps = _dot(ys_ref[0], wps_ref[...])
    pa = _dot(ya_ref[0], wpa_ref[...])
    merged = g[:, 0:D_MODEL] * ps + g[:, D_MODEL:2 * D_MODEL] * pa
    out = _dot(merged.astype(BF16), wo_ref[...])
    xn = x_ref[0] + gate_ref[0] * out
    ms = jnp.mean(xn * xn, axis=-1, keepdims=True)
    o_ref[0] = xn * lax.rsqrt(ms + EPS) * fnw_ref[...]


def _outproj(y_ssm, y_att, p, x, gate, b_gate, wps, wpa, wo, fnw):
    b, s, d = x.shape
    tm = OUT_TM
    const = lambda shape: pl.BlockSpec(shape, lambda bi, i: tuple(0 for _ in shape))
    return pl.pallas_call(
        _out_kernel,
        grid=(b, s // tm),
        in_specs=[
            pl.BlockSpec((1, tm, d), lambda bi, i: (bi, i, 0)),
            pl.BlockSpec((1, tm, d), lambda bi, i: (bi, i, 0)),
            pl.BlockSpec((1, tm, 2 * d), lambda bi, i: (bi, i, COL_G // (2 * d))),
            pl.BlockSpec((1, tm, d), lambda bi, i: (bi, i, 0)),
            pl.BlockSpec((1, 1, d), lambda bi, i: (bi, 0, 0)),
            const((1, 2 * d)), const((d, d)), const((d, d)), const((d, d)), const((1, d)),
        ],
        out_specs=pl.BlockSpec((1, tm, d), lambda bi, i: (bi, i, 0)),
        out_shape=jax.ShapeDtypeStruct((b, s, d), F32),
        compiler_params=pltpu.CompilerParams(
            dimension_semantics=("arbitrary", "arbitrary"), vmem_limit_bytes=VMEM_LIMIT),
        name="outproj",
    )(y_ssm, y_att, p, x, gate, b_gate, wps, wpa, wo, fnw)


def _constants():
    tril = np.tril(np.ones((CHUNK, CHUNK), np.float32))
    ee = np.zeros((2 * LANES, SSM_INNER), np.float32)
    for h in range(SSM_HEADS):
        ee[h, h * SSM_HEADDIM:(h + 1) * SSM_HEADDIM] = 1.0
        ee[LANES + h, h * SSM_HEADDIM:(h + 1) * SSM_HEADDIM] = 1.0
    eq = np.zeros((3 * LANES, LANES), np.float32)
    ek = np.zeros((3 * LANES, LANES), np.float32)
    for h in range(ATT_HEADS):
        base = h * AUG_W
        for part in range(3):
            eq[part * LANES + F_LANE0 + h, base + 3 + part] = 1.0
            ek[part * LANES + F_LANE0 + h, base + part] = -1.0
            eq[ONE_LANE, base + part] = 1.0
            ek[ONE_LANE, base + 3 + part] = 1.0
    as_bf16 = lambda a: jnp.asarray(a, dtype=BF16)
    return as_bf16(tril), as_bf16(ee), as_bf16(eq), as_bf16(ek)


def kernel(x, c, w_ada, b_ada, norm_w, w_in, conv_w, conv_b, dt_bias, a_log, d_skip,
           ssm_norm_w, b_f, b_gate, w_proj_ssm, w_proj_att, w_out, final_norm_w):
    b, s, d = x.shape
    row = lambda v: v.reshape(1, -1).astype(F32)

    c_pad = jnp.zeros((8, d), F32).at[0:b].set(c)
    ada = _ada(c_pad, w_ada, row(b_ada))[0:b]
    shift = ada[:, 0:d].reshape(b, 1, d)
    scale = ada[:, d:2 * d].reshape(b, 1, d)
    gate = ada[:, 2 * d:3 * d].reshape(b, 1, d)

    o_z, o_xbc, o_dt, o_q, o_k, o_v, o_za, o_f, o_g = np.cumsum(
        [0, SSM_INNER, SSM_INNER + BC_W, SSM_HEADS, ATT_INNER, ATT_INNER, ATT_INNER,
         ATT_INNER, ATT_HEADS]).tolist()
    w_bf = w_in.astype(BF16)
    w_big = jnp.concatenate([
        w_bf[:, o_z:o_z + SSM_INNER],
        w_bf[:, o_xbc:o_xbc + SSM_INNER],
        w_bf[:, o_q:o_q + ATT_INNER],
        w_bf[:, o_k:o_k + ATT_INNER],
        w_bf[:, o_v:o_v + ATT_INNER],
        w_bf[:, o_za:o_za + ATT_INNER],
        w_bf[:, o_g:o_g + 2 * d],
        w_bf[:, o_xbc + SSM_INNER:o_xbc + SSM_INNER + BC_W],
    ], axis=1)
    w_sm = jnp.concatenate([
        w_in[:, o_dt:o_dt + SSM_HEADS], w_in[:, o_f:o_f + ATT_HEADS],
        jnp.zeros((d, LANES - SSM_HEADS - ATT_HEADS), F32)], axis=1)
    w_sm_hi = w_sm.astype(BF16)
    w_sm_lo = (w_sm - w_sm_hi.astype(F32)).astype(BF16)
    w_small = jnp.concatenate([w_sm_hi, w_sm_lo], axis=1)

    p, sm = _inproj(x, row(norm_w), scale, shift, w_big, w_small,
                    conv_w.astype(F32), row(conv_b))

    tril, ee, eq, ek = _constants()
    pad_lanes = lambda v, lane0: jnp.zeros((1, LANES), F32).at[0, lane0:lane0 + v.shape[0]].set(v)
    bias_row = pad_lanes(dt_bias.astype(F32), DT_LANE0) + pad_lanes(b_f.astype(F32), F_LANE0)
    alog_row = pad_lanes(a_log.astype(F32), DT_LANE0)
    dskip_e = jnp.repeat(d_skip.astype(F32), SSM_HEADDIM).reshape(1, SSM_INNER)
    y_ssm, qa, ka = _ssd(
        p, sm, bias_row, alog_row, dskip_e, row(ssm_norm_w), tril, ee, eq, ek)

    y_att = _attn(p, qa, ka)

    return _outproj(y_ssm, y_att, p, x, gate, row(b_gate),
                    w_proj_ssm.astype(BF16), w_proj_att.astype(BF16), w_out.astype(BF16),
                    row(final_norm_w))
```

```python
import functools

import jax
import jax.numpy as jnp
import numpy as np
from jax import lax
from jax.experimental import pallas as pl
from jax.experimental.pallas import tpu as pltpu

F32 = jnp.float32
BF16 = jnp.bfloat16

D_MODEL = 1024
SSM_HEADDIM = 64
SSM_HEADS = 16
SSM_INNER = SSM_HEADS * SSM_HEADDIM
SSM_GROUPS = 2
SSM_STATE = 64
CONV_K = 4
CHUNK = 128
ATT_HEADS = 8
ATT_HEADDIM = 128
ATT_INNER = ATT_HEADS * ATT_HEADDIM
EPS = 1e-6
LOG2E = 1.4426950408889634

LANES = 128
GROUP_W = SSM_INNER // SSM_GROUPS
BC_W = 2 * SSM_GROUPS * SSM_STATE

COL_Z_SSM = 0
COL_XS = 1024
COL_Q = 2048
COL_K = 3072
COL_V = 4096
COL_Z_ATT = 5120
COL_G = 6144
COL_BC = 8192
P_COLS = 8448
P_GROUPS = ((COL_Z_SSM, SSM_INNER), (COL_XS, SSM_INNER), (COL_Q, ATT_INNER), (COL_K, ATT_INNER),
            (COL_V, ATT_INNER), (COL_Z_ATT, ATT_INNER), (COL_G, 2 * D_MODEL), (COL_BC, BC_W))

DT_LANE0 = 0
F_LANE0 = SSM_HEADS
ONE_LANE = LANES - 1
AUG_W = LANES // ATT_HEADS

VMEM_LIMIT = 56 * 1024 * 1024


def _dot(a, b):
    return jnp.dot(a, b, preferred_element_type=F32)


def _split2(x):
    hi = x.astype(BF16)
    lo = (x - hi.astype(F32)).astype(BF16)
    return hi, lo


def _split3(x):
    hi = x.astype(BF16)
    r1 = x - hi.astype(F32)
    mid = r1.astype(BF16)
    lo = (r1 - mid.astype(F32)).astype(BF16)
    return hi, mid, lo


def _sigmoid(x):
    return 0.5 * jnp.tanh(0.5 * x) + 0.5


def _silu(x):
    h = 0.5 * x
    return h + h * jnp.tanh(h)


def _ada_kernel(c_ref, w_ref, b_ref, o_ref):
    c = c_ref[...]
    cs = c * jax.nn.sigmoid(c)
    w = w_ref[...]
    c_hi, c_lo = _split2(cs)
    w_hi, w_lo = _split2(w)
    o_ref[...] = _dot(c_hi, w_hi) + _dot(c_lo, w_hi) + _dot(c_hi, w_lo) + b_ref[...]


def _ada(c_pad, w_ada, b_ada):
    rows = c_pad.shape[0]
    n = w_ada.shape[1]
    tn = 1024
    return pl.pallas_call(
        _ada_kernel,
        grid=(n // tn,),
        in_specs=[
            pl.BlockSpec((rows, D_MODEL), lambda j: (0, 0)),
            pl.BlockSpec((D_MODEL, tn), lambda j: (0, j)),
            pl.BlockSpec((1, tn), lambda j: (0, j)),
        ],
        out_specs=pl.BlockSpec((rows, tn), lambda j: (0, j)),
        out_shape=jax.ShapeDtypeStruct((rows, n), F32),
        compiler_params=pltpu.CompilerParams(
            dimension_semantics=("arbitrary",), vmem_limit_bytes=VMEM_LIMIT),
        name="ada",
    )(c_pad, w_ada, b_ada)


INPROJ_TM = 512
INPROJ_CW = 256
TAIL = 8
Q_SCALE = ATT_HEADDIM ** -0.5 * LOG2E


def _inproj_kernel(x_ref, nw_ref, scale_ref, shift_ref, *refs):
    tm = INPROJ_TM
    w_refs = refs[:len(P_GROUPS)]
    ws_ref, cw_ref, cb_ref, p_ref, sm_ref, ext_ref = refs[len(P_GROUPS):]

    @pl.when(pl.program_id(1) == 0)
    def _():
        ext_ref[:, 0:TAIL, :] = jnp.zeros((ext_ref.shape[0], TAIL, LANES), F32)

    x = x_ref[0]
    ms = jnp.mean(x * x, axis=-1, keepdims=True)
    y = x * lax.rsqrt(ms + EPS) * nw_ref[...]
    h = y * (1.0 + scale_ref[0]) + shift_ref[0]
    h_hi, h_lo = _split2(h)
    hh = _dot(h_hi, ws_ref[...])
    sm_ref[0] = hh[:, 0:LANES] + hh[:, LANES:2 * LANES] + _dot(h_lo, ws_ref[:, 0:LANES])

    def conv_channel(col):
        if COL_XS <= col < COL_XS + SSM_INNER:
            return col - COL_XS
        if COL_BC <= col < COL_BC + BC_W:
            return SSM_INNER + col - COL_BC
        return None

    chunks = [(w_ref, gcol, off) for w_ref, (gcol, gw) in zip(w_refs, P_GROUPS)
              for off in range(0, gw, INPROJ_CW)]
    for w_ref, gcol, off in chunks:
        col0 = gcol + off
        acc = _dot(h_hi, w_ref[:, off:off + INPROJ_CW])
        if COL_Q <= col0 < COL_Q + ATT_INNER:
            acc = acc * Q_SCALE
        if conv_channel(col0) is None:
            p_ref[0, :, col0:col0 + INPROJ_CW] = acc.astype(BF16)
            continue
        for t in range(INPROJ_CW // LANES):
            ch0 = conv_channel(col0 + t * LANES)
            e = ch0 // LANES
            chs = slice(ch0, ch0 + LANES)
            u = acc[:, t * LANES:(t + 1) * LANES]
            ext_ref[e, TAIL:TAIL + tm, :] = u
            a = cb_ref[:, chs] + cw_ref[CONV_K - 1:CONV_K, chs] * u
            for k in range(CONV_K - 1):
                off = TAIL - (CONV_K - 1) + k
                a = a + cw_ref[k:k + 1, chs] * ext_ref[e, off:off + tm, :]
            p_ref[0, :, col0 + t * LANES:col0 + (t + 1) * LANES] = _silu(a).astype(BF16)
            ext_ref[e, 0:TAIL, :] = ext_ref[e, tm:tm + TAIL, :]


def _inproj(x, norm_w, scale, shift, w_groups, w_small, conv_w, conv_b):
    b, s, d = x.shape
    tm = INPROJ_TM
    conv_dim = SSM_INNER + BC_W
    resident = lambda shape: pl.BlockSpec(shape, lambda bi, i: tuple(0 for _ in shape),
                                          pipeline_mode=pl.Buffered(1))
    return pl.pallas_call(
        _inproj_kernel,
        grid=(b, s // tm),
        in_specs=[
            pl.BlockSpec((1, tm, d), lambda bi, i: (bi, i, 0)),
            pl.BlockSpec((1, d), lambda bi, i: (0, 0)),
            pl.BlockSpec((1, 1, d), lambda bi, i: (bi, 0, 0)),
            pl.BlockSpec((1, 1, d), lambda bi, i: (bi, 0, 0)),
            *[resident((d, gw)) for _, gw in P_GROUPS],
            resident((d, 2 * LANES)),
            pl.BlockSpec((CONV_K, conv_dim), lambda bi, i: (0, 0)),
            pl.BlockSpec((1, conv_dim), lambda bi, i: (0, 0)),
        ],
        out_specs=[
            pl.BlockSpec((1, tm, P_COLS), lambda bi, i: (bi, i, 0)),
            pl.BlockSpec((1, tm, LANES), lambda bi, i: (bi, i, 0)),
        ],
        out_shape=[
            jax.ShapeDtypeStruct((b, s, P_COLS), BF16),
            jax.ShapeDtypeStruct((b, s, LANES), F32),
        ],
        scratch_shapes=[pltpu.VMEM((conv_dim // LANES, tm + TAIL, LANES), F32)],
        compiler_params=pltpu.CompilerParams(
            dimension_semantics=("arbitrary", "arbitrary"), vmem_limit_bytes=VMEM_LIMIT),
        name="inproj",
    )(x, norm_w, scale, shift, *w_groups, w_small, conv_w, conv_b)


SSD_ROWS = 512
SSD_UNROLL = 2


def _ssd_kernel(z_ref, xs_ref, bc_ref, sm_ref,
                bias_ref, alog_ref, dskip_ref, nw_ref, tril_ref, ee_ref, eq_ref, ek_ref,
                y_ref, qa_ref, ka_ref,
                state_ref, carry_ref):
    rows = SSD_ROWS
    L = CHUNK

    @pl.when(pl.program_id(1) == 0)
    def _():
        state_ref[...] = jnp.zeros(state_ref.shape, F32)
        carry_ref[...] = jnp.zeros(carry_ref.shape, F32)

    lane = lax.broadcasted_iota(jnp.int32, (L, LANES), 1)
    is_dt = lane < SSM_HEADS
    is_f = jnp.logical_and(lane >= F_LANE0, lane < F_LANE0 + ATT_HEADS)
    row_i = lax.broadcasted_iota(jnp.int32, (L, L), 0)
    col_i = lax.broadcasted_iota(jnp.int32, (L, L), 1)
    causal = row_i >= col_i
    neg_a = -jnp.exp(alog_ref[...]) * LOG2E
    tril = tril_ref[...]

    def chunk(c):
        r0 = pl.multiple_of(c * L, L)
        v = sm_ref[0, pl.ds(r0, L), :] + bias_ref[...]
        t = jnp.log(1.0 + jnp.exp(-jnp.abs(v)))
        dt = jnp.maximum(v, 0.0) + t
        lf = jnp.minimum(v, 0.0) - t
        val = jnp.where(is_dt, dt * neg_a, jnp.where(is_f, lf, 0.0))
        v_hi, v_mid, v_lo = _split3(val)
        cs3 = _dot(tril, jnp.concatenate([v_hi, v_mid, v_lo], axis=1))
        cs = cs3[:, 0:LANES] + cs3[:, LANES:2 * LANES] + cs3[:, 2 * LANES:3 * LANES]

        fc = jnp.where(is_f, cs + carry_ref[...], 0.0)
        carry_ref[...] = fc[L - 1:L, :]
        fa = jnp.where(lane == ONE_LANE, 1.0, fc * LOG2E)
        f_hi, f_mid, f_lo = _split3(fa)
        f3 = jnp.concatenate([f_hi, f_mid, f_lo], axis=1)
        qa_ref[0, pl.ds(r0, L), :] = _dot(f3, eq_ref[...]).astype(BF16)
        ka_ref[0, pl.ds(r0, L), :] = _dot(f3, ek_ref[...]).astype(BF16)

        a_cs = jnp.where(is_dt, cs, 0.0)
        a_last = a_cs[L - 1:L, :]
        dtm = jnp.where(is_dt, dt, 0.0)
        w2 = jnp.where(is_dt, jnp.exp2(a_cs), 0.0)
        w1 = dtm * jnp.exp2(a_last - a_cs)
        st_hi, st_lo = _split2(jnp.concatenate([dtm, w2, w1], axis=0))
        ex = _dot(jnp.concatenate([st_hi, st_lo], axis=1), ee_ref[...])
        dt_e = ex[0:L]
        w2_e = ex[L:2 * L]
        w1_e = ex[2 * L:3 * L]

        xsc = xs_ref[0, pl.ds(r0, L), :].astype(F32)
        bcv = bc_ref[0, pl.ds(r0, L), :].astype(F32)
        b_all = bcv[:, 0:LANES]
        c_all = bcv[:, LANES:2 * LANES]
        b_t = b_all.T.astype(BF16)
        xd = xsc * dt_e
        xw1 = (xsc * w1_e).astype(BF16)
        a_cs_t = a_cs.T

        ys = []
        for g in range(SSM_GROUPS):
            gsl = slice(g * GROUP_W, (g + 1) * GROUP_W)
            in_g = jnp.logical_and(lane >= g * SSM_STATE, lane < (g + 1) * SSM_STATE)
            c_m = jnp.where(in_g, c_all, 0.0).astype(BF16)
            cb = jnp.where(causal, _dot(c_m, b_t), 0.0).astype(BF16)
            s_g = state_ref[g]
            y_off = _dot(c_m, s_g.astype(BF16)) * w2_e[:, gsl]
            pieces = []
            for j in range(GROUP_W // LANES):
                h_a = g * (SSM_HEADS // SSM_GROUPS) + 2 * j
                xp = xd[:, g * GROUP_W + j * LANES:g * GROUP_W + (j + 1) * LANES]
                x_a = jnp.where(lane < SSM_HEADDIM, xp, 0.0).astype(BF16)
                x_b = jnp.where(lane >= SSM_HEADDIM, xp, 0.0).astype(BF16)

                def lmat(h):
                    seg = (a_cs[:, h:h + 1] - a_cs_t[h:h + 1, :]).astype(BF16)
                    return cb * jnp.exp2(jnp.minimum(seg, 0))

                pieces.append(_dot(lmat(h_a), x_a) + _dot(lmat(h_a + 1), x_b))
            ys.append(jnp.concatenate(pieces, axis=1) + y_off)
            state_ref[g] = s_g * w2_e[L - 1:L, gsl] + _dot(b_t, xw1[:, gsl])

        y = jnp.concatenate(ys, axis=1) + dskip_ref[...] * xsc
        z = z_ref[0, pl.ds(r0, L), :].astype(F32)
        yg = y * _silu(z)
        ms = jnp.mean(yg * yg, axis=-1, keepdims=True)
        y_ref[0, pl.ds(r0, L), :] = (yg * lax.rsqrt(ms + EPS) * nw_ref[...]).astype(BF16)

    def chunks(i, carry):
        for u in range(SSD_UNROLL):
            chunk(SSD_UNROLL * i + u)
        return carry

    lax.fori_loop(0, rows // L // SSD_UNROLL, chunks, 0)


def _ssd(p, sm, bias_row, alog_row, dskip_e, nw, tril, ee, eq, ek):
    b, s, _ = p.shape
    rows = SSD_ROWS
    const = lambda shape: pl.BlockSpec(shape, lambda bi, i: tuple(0 for _ in shape))
    out_sds = jax.ShapeDtypeStruct((b, s, SSM_INNER), BF16)
    aug_sds = jax.ShapeDtypeStruct((b, s, LANES), BF16)
    return pl.pallas_call(
        _ssd_kernel,
        grid=(b, s // rows),
        in_specs=[
            pl.BlockSpec((1, rows, SSM_INNER), lambda bi, i: (bi, i, COL_Z_SSM // SSM_INNER)),
            pl.BlockSpec((1, rows, SSM_INNER), lambda bi, i: (bi, i, COL_XS // SSM_INNER)),
            pl.BlockSpec((1, rows, BC_W), lambda bi, i: (bi, i, COL_BC // BC_W)),
            pl.BlockSpec((1, rows, LANES), lambda bi, i: (bi, i, 0)),
            const((1, LANES)), const((1, LANES)),
            const((1, SSM_INNER)), const((1, SSM_INNER)),
            const((CHUNK, CHUNK)), const((2 * LANES, SSM_INNER)),
            const((3 * LANES, LANES)), const((3 * LANES, LANES)),
        ],
        out_specs=[
            pl.BlockSpec((1, rows, SSM_INNER), lambda bi, i: (bi, i, 0)),
            pl.BlockSpec((1, rows, LANES), lambda bi, i: (bi, i, 0)),
            pl.BlockSpec((1, rows, LANES), lambda bi, i: (bi, i, 0)),
        ],
        out_shape=[out_sds, aug_sds, aug_sds],
        scratch_shapes=[
            pltpu.VMEM((SSM_GROUPS, LANES, GROUP_W), F32),
            pltpu.VMEM((1, LANES), F32),
        ],
        compiler_params=pltpu.CompilerParams(
            dimension_semantics=("arbitrary", "arbitrary"), vmem_limit_bytes=VMEM_LIMIT),
        name="ssd",
    )(p, p, p, sm, bias_row, alog_row, dskip_e, nw, tril, ee, eq, ek)


ATT_TQ = 512
ATT_TK = ATT_TQ
ATT_SPARE_SLOT = 2
ATT_HPS = 2
ATT_QSUB = 8
ATT_UNROLL = 4


def _attn_kernel(q_ref, qa_ref, k_ref, ka_ref, v_ref, z_ref, o_ref,
                 kam_ref, s_ref, mx_ref, m_ref, acc_ref):
    tk = ATT_TK

    @pl.when(pl.program_id(2) == 0)
    def _():
        lane_head = lax.broadcasted_iota(jnp.int32, (tk, LANES), 1) // AUG_W
        for hh in range(ATT_HPS):
            own = lane_head == pl.program_id(1) * ATT_HPS + hh

            def fill(t, carry):
                r0 = pl.multiple_of(t * tk, tk)
                blk = ka_ref[0, pl.ds(r0, tk), :].astype(F32)
                kam_ref[hh, pl.ds(r0, tk), :] = jnp.where(own, blk, 0.0).astype(BF16)
                return carry

            lax.fori_loop(0, ka_ref.shape[1] // tk, fill, 0)

    refs = (q_ref, qa_ref, k_ref, kam_ref, v_ref, z_ref, o_ref, s_ref, mx_ref, m_ref, acc_ref)
    _attn_first_scores(_attn_q_operand(0, q_ref, qa_ref), k_ref, kam_ref, s_ref, mx_ref)

    def sub_tile(sub, carry):
        _attn_q_tile(sub, *refs)
        return carry

    lax.fori_loop(0, ATT_QSUB, sub_tile, 0)


def _attn_q_operand(sub, q_ref, qa_ref):
    hd = ATT_HEADDIM
    rows = pl.ds(pl.multiple_of(sub * ATT_TQ, ATT_TQ), ATT_TQ)
    qa = qa_ref[0, rows, :]
    return [jnp.concatenate([q_ref[0, rows, hh * hd:(hh + 1) * hd], qa], axis=1)
            for hh in range(ATT_HPS)]


def _attn_first_scores(qq, k_ref, kam_ref, s_ref, mx_ref):
    hd = ATT_HEADDIM
    for hh in range(ATT_HPS):
        kk = jnp.concatenate([k_ref[0, 0:ATT_TK, hh * hd:(hh + 1) * hd], kam_ref[hh, 0:ATT_TK, :]],
                             axis=1)
        s = lax.dot_general(qq[hh], kk, (((1,), (1,)), ((), ())), preferred_element_type=F32)
        s_ref[0, hh] = s
        mx_ref[0, hh] = jnp.broadcast_to(jnp.max(s, axis=1, keepdims=True), (ATT_TQ, LANES))


def _attn_q_tile(sub, q_ref, qa_ref, k_ref, kam_ref, v_ref, z_ref, o_ref,
                 s_ref, mx_ref, m_ref, acc_ref):
    tq, tk, hd = ATT_TQ, ATT_TK, ATT_HEADDIM
    heads = range(ATT_HPS)
    hsl = lambda hh: slice(hh * hd, (hh + 1) * hd)
    qi = pl.program_id(2) * ATT_QSUB + sub
    qrows = pl.ds(pl.multiple_of(sub * tq, tq), tq)

    qq = _attn_q_operand(sub, q_ref, qa_ref)
    m_ref[...] = jnp.full(m_ref.shape, -jnp.inf, F32)
    acc_ref[...] = jnp.zeros(acc_ref.shape, F32)
    ones_v = jnp.ones((tk, hd), BF16)

    def scores(j, slot):
        r0 = pl.multiple_of(j * tk, tk)
        for hh in heads:
            kk = jnp.concatenate(
                [k_ref[0, pl.ds(r0, tk), hsl(hh)], kam_ref[hh, pl.ds(r0, tk), :]], axis=1)
            s = lax.dot_general(qq[hh], kk, (((1,), (1,)), ((), ())), preferred_element_type=F32)
            s_ref[slot, hh] = s
            mx_ref[slot, hh] = jnp.broadcast_to(jnp.max(s, axis=1, keepdims=True), (tq, LANES))

    def update(j, slot, masked=False):
        r0 = pl.multiple_of(j * tk, tk)
        for hh in heads:
            s = s_ref[slot, hh]
            if masked:
                row = lax.broadcasted_iota(jnp.int32, (tq, tk), 0)
                col = lax.broadcasted_iota(jnp.int32, (tq, tk), 1)
                s = jnp.where(row >= col, s, -jnp.inf)
                mx = jnp.broadcast_to(jnp.max(s, axis=1, keepdims=True), (tq, LANES))
            else:
                mx = mx_ref[slot, hh]
            m_old = m_ref[hh]
            m_new = jnp.maximum(m_old, mx)
            alpha = jnp.exp2(m_old - m_new)
            p = jnp.concatenate(
                [jnp.exp2((s[:, c * LANES:(c + 1) * LANES] - m_new).astype(BF16))
                 for c in range(tk // LANES)], axis=1)
            vv = jnp.concatenate([v_ref[0, pl.ds(r0, tk), hsl(hh)], ones_v], axis=1)
            pv = _dot(p, vv)
            acc_ref[hh, :, 0:hd] = alpha * acc_ref[hh, :, 0:hd] + pv[:, 0:hd]
            acc_ref[hh, :, hd:2 * hd] = alpha * acc_ref[hh, :, hd:2 * hd] + pv[:, hd:2 * hd]
            m_ref[hh] = m_new

    def run(j0, n, last_slot=None):
        for u in range(n):
            last = u == n - 1 and last_slot is not None
            scores(j0 + u + 1, last_slot if last else (u + 1) % 2)
            update(j0 + u, u % 2)

    def body(i, carry):
        run(ATT_UNROLL * i, ATT_UNROLL)
        return carry

    def next_first_scores():
        nxt = jnp.minimum(sub + 1, ATT_QSUB - 1)
        _attn_first_scores(_attn_q_operand(nxt, q_ref, qa_ref), k_ref, kam_ref, s_ref, mx_ref)

    lax.fori_loop(0, qi // ATT_UNROLL, body, 0)
    for r in range(ATT_UNROLL):
        @pl.when(qi % ATT_UNROLL == r)
        def _(r=r):
            if r == 0:
                update(qi, 0, masked=True)
                next_first_scores()
            else:
                run(qi - r, r, last_slot=ATT_SPARE_SLOT)
                next_first_scores()
                update(qi, ATT_SPARE_SLOT, masked=True)

    for hh in heads:
        z = z_ref[0, qrows, hsl(hh)].astype(F32)
        o = acc_ref[hh, :, 0:hd] / acc_ref[hh, :, hd:2 * hd]
        o_ref[0, qrows, hsl(hh)] = (o * _silu(z)).astype(BF16)


def _attn(p, qa, ka):
    b, s, _ = p.shape
    tq = ATT_TQ * ATT_QSUB
    w = ATT_HPS * ATT_HEADDIM
    return pl.pallas_call(
        _attn_kernel,
        grid=(b, ATT_HEADS // ATT_HPS, s // tq),
        in_specs=[
            pl.BlockSpec((1, tq, w), lambda bi, h, i: (bi, i, COL_Q // w + h)),
            pl.BlockSpec((1, tq, LANES), lambda bi, h, i: (bi, i, 0)),
            pl.BlockSpec((1, s, w), lambda bi, h, i: (bi, 0, COL_K // w + h)),
            pl.BlockSpec((1, s, LANES), lambda bi, h, i: (bi, 0, 0)),
            pl.BlockSpec((1, s, w), lambda bi, h, i: (bi, 0, COL_V // w + h)),
            pl.BlockSpec((1, tq, w), lambda bi, h, i: (bi, i, COL_Z_ATT // w + h)),
        ],
        out_specs=pl.BlockSpec((1, tq, w), lambda bi, h, i: (bi, i, h)),
        out_shape=jax.ShapeDtypeStruct((b, s, ATT_INNER), BF16),
        scratch_shapes=[
            pltpu.VMEM((ATT_HPS, s, LANES), BF16),
            pltpu.VMEM((3, ATT_HPS, ATT_TQ, ATT_TK), F32),
            pltpu.VMEM((3, ATT_HPS, ATT_TQ, LANES), F32),
            pltpu.VMEM((ATT_HPS, ATT_TQ, LANES), F32),
            pltpu.VMEM((ATT_HPS, ATT_TQ, 2 * ATT_HEADDIM), F32),
        ],
        compiler_params=pltpu.CompilerParams(
            dimension_semantics=("arbitrary", "arbitrary", "arbitrary"),
            vmem_limit_bytes=VMEM_LIMIT),
        name="fox_attn",
    )(p, qa, p, ka, p, p)


OUT_TM = 1024


def _out_kernel(ys_ref, ya_ref, g_ref, x_ref, gate_ref, bg_ref, wps_ref, wpa_ref, wo_ref,
                fnw_ref, o_ref):
    g = _sigmoid(g_ref[0].astype(F32) + bg_ref[...])
    ps = _dot(ys_ref[0], wps_ref[...])
    pa = _dot(ya_ref[0], wpa_ref[...])
    merged = g[:, 0:D_MODEL] * ps + g[:, D_MODEL:2 * D_MODEL] * pa
    out = _dot(merged.astype(BF16), wo_ref[...])
    xn = x_ref[0] + gate_ref[0] * out
    ms = jnp.mean(xn * xn, axis=-1, keepdims=True)
    o_ref[0] = xn * lax.rsqrt(ms + EPS) * fnw_ref[...]


def _outproj(y_ssm, y_att, p, x, gate, b_gate, wps, wpa, wo, fnw):
    b, s, d = x.shape
    tm = OUT_TM
    const = lambda shape: pl.BlockSpec(shape, lambda bi, i: tuple(0 for _ in shape))
    return pl.pallas_call(
        _out_kernel,
        grid=(b, s // tm),
        in_specs=[
            pl.BlockSpec((1, tm, d), lambda bi, i: (bi, i, 0)),
            pl.BlockSpec((1, tm, d), lambda bi, i: (bi, i, 0)),
            pl.BlockSpec((1, tm, 2 * d), lambda bi, i: (bi, i, COL_G // (2 * d))),
            pl.BlockSpec((1, tm, d), lambda bi, i: (bi, i, 0)),
            pl.BlockSpec((1, 1, d), lambda bi, i: (bi, 0, 0)),
            const((1, 2 * d)), const((d, d)), const((d, d)), const((d, d)), const((1, d)),
        ],
        out_specs=pl.BlockSpec((1, tm, d), lambda bi, i: (bi, i, 0)),
        out_shape=jax.ShapeDtypeStruct((b, s, d), F32),
        compiler_params=pltpu.CompilerParams(
            dimension_semantics=("arbitrary", "arbitrary"), vmem_limit_bytes=VMEM_LIMIT),
        name="outproj",
    )(y_ssm, y_att, p, x, gate, b_gate, wps, wpa, wo, fnw)


def _constants():
    tril = np.tril(np.ones((CHUNK, CHUNK), np.float32))
    ee = np.zeros((2 * LANES, SSM_INNER), np.float32)
    for h in range(SSM_HEADS):
        ee[h, h * SSM_HEADDIM:(h + 1) * SSM_HEADDIM] = 1.0
        ee[LANES + h, h * SSM_HEADDIM:(h + 1) * SSM_HEADDIM] = 1.0
    eq = np.zeros((3 * LANES, LANES), np.float32)
    ek = np.zeros((3 * LANES, LANES), np.float32)
    for h in range(ATT_HEADS):
        base = h * AUG_W
        for part in range(3):
            eq[part * LANES + F_LANE0 + h, base + 3 + part] = 1.0
            ek[part * LANES + F_LANE0 + h, base + part] = -1.0
            eq[ONE_LANE, base + part] = 1.0
            ek[ONE_LANE, base + 3 + part] = 1.0
    as_bf16 = lambda a: jnp.asarray(a, dtype=BF16)
    return as_bf16(tril), as_bf16(ee), as_bf16(eq), as_bf16(ek)


def kernel(x, c, w_ada, b_ada, norm_w, w_in, conv_w, conv_b, dt_bias, a_log, d_skip,
           ssm_norm_w, b_f, b_gate, w_proj_ssm, w_proj_att, w_out, final_norm_w):
    b, s, d = x.shape
    row = lambda v: v.reshape(1, -1).astype(F32)

    c_pad = jnp.zeros((8, d), F32).at[0:b].set(c)
    ada = _ada(c_pad, w_ada, row(b_ada))[0:b]
    shift = ada[:, 0:d].reshape(b, 1, d)
    scale = ada[:, d:2 * d].reshape(b, 1, d)
    gate = ada[:, 2 * d:3 * d].reshape(b, 1, d)

    o_z, o_xbc, o_dt, o_q, o_k, o_v, o_za, o_f, o_g = np.cumsum(
        [0, SSM_INNER, SSM_INNER + BC_W, SSM_HEADS, ATT_INNER, ATT_INNER, ATT_INNER,
         ATT_INNER, ATT_HEADS]).tolist()
    group_src = (o_z, o_xbc, o_q, o_k, o_v, o_za, o_g, o_xbc + SSM_INNER)
    w_groups = [w_in[:, src:src + gw].astype(BF16) for src, (_, gw) in zip(group_src, P_GROUPS)]
    w_sm = jnp.concatenate([
        w_in[:, o_dt:o_dt + SSM_HEADS], w_in[:, o_f:o_f + ATT_HEADS],
        jnp.zeros((d, LANES - SSM_HEADS - ATT_HEADS), F32)], axis=1)
    w_sm_hi = w_sm.astype(BF16)
    w_sm_lo = (w_sm - w_sm_hi.astype(F32)).astype(BF16)
    w_small = jnp.concatenate([w_sm_hi, w_sm_lo], axis=1)

    p, sm = _inproj(x, row(norm_w), scale, shift, w_groups, w_small,
                    conv_w.astype(F32), row(conv_b))

    tril, ee, eq, ek = _constants()
    pad_lanes = lambda v, lane0: jnp.zeros((1, LANES), F32).at[0, lane0:lane0 + v.shape[0]].set(v)
    bias_row = pad_lanes(dt_bias.astype(F32), DT_LANE0) + pad_lanes(b_f.astype(F32), F_LANE0)
    alog_row = pad_lanes(a_log.astype(F32), DT_LANE0)
    dskip_e = jnp.repeat(d_skip.astype(F32), SSM_HEADDIM).reshape(1, SSM_INNER)
    y_ssm, qa, ka = _ssd(
        p, sm, bias_row, alog_row, dskip_e, row(ssm_norm_w), tril, ee, eq, ek)

    y_att = _attn(p, qa, ka)

    return _outproj(y_ssm, y_att, p, x, gate, row(b_gate),
                    w_proj_ssm.astype(BF16), w_proj_att.astype(BF16), w_out.astype(BF16),
                    row(final_norm_w))
```

```python
import functools

import jax
import jax.numpy as jnp
import numpy as np
from jax import lax
from jax.experimental import pallas as pl
from jax.experimental.pallas import tpu as pltpu

F32 = jnp.float32
BF16 = jnp.bfloat16

D_MODEL = 1024
SSM_HEADDIM = 64
SSM_HEADS = 16
SSM_INNER = SSM_HEADS * SSM_HEADDIM
SSM_GROUPS = 2
SSM_STATE = 64
CONV_K = 4
CHUNK = 128
ATT_HEADS = 8
ATT_HEADDIM = 128
ATT_INNER = ATT_HEADS * ATT_HEADDIM
EPS = 1e-6
LOG2E = 1.4426950408889634

LANES = 128
GROUP_W = SSM_INNER // SSM_GROUPS
BC_W = 2 * SSM_GROUPS * SSM_STATE

COL_Z_SSM = 0
COL_XS = 1024
COL_Q = 2048
COL_K = 3072
COL_V = 4096
COL_Z_ATT = 5120
COL_G = 6144
COL_BC = 8192
P_COLS = 8448
P_OUT0 = COL_Q
P_OUT_COLS = COL_BC - COL_Q

DT_LANE0 = 0
F_LANE0 = SSM_HEADS
ONE_LANE = LANES - 1
AUG_W = LANES // ATT_HEADS

VMEM_LIMIT = 56 * 1024 * 1024


def _dot(a, b):
    return jnp.dot(a, b, preferred_element_type=F32)


def _split2(x):
    hi = x.astype(BF16)
    lo = (x - hi.astype(F32)).astype(BF16)
    return hi, lo


def _split3(x):
    hi = x.astype(BF16)
    r1 = x - hi.astype(F32)
    mid = r1.astype(BF16)
    lo = (r1 - mid.astype(F32)).astype(BF16)
    return hi, mid, lo


def _sigmoid(x):
    return 0.5 * jnp.tanh(0.5 * x) + 0.5


def _silu(x):
    h = 0.5 * x
    return h + h * jnp.tanh(h)


def _ada_kernel(c_ref, w_ref, b_ref, o_ref):
    c = c_ref[...]
    cs = c * jax.nn.sigmoid(c)
    w = w_ref[...]
    c_hi, c_lo = _split2(cs)
    w_hi, w_lo = _split2(w)
    o_ref[...] = _dot(c_hi, w_hi) + _dot(c_lo, w_hi) + _dot(c_hi, w_lo) + b_ref[...]


def _ada(c_pad, w_ada, b_ada):
    rows = c_pad.shape[0]
    n = w_ada.shape[1]
    tn = 1024
    return pl.pallas_call(
        _ada_kernel,
        grid=(n // tn,),
        in_specs=[
            pl.BlockSpec((rows, D_MODEL), lambda j: (0, 0)),
            pl.BlockSpec((D_MODEL, tn), lambda j: (0, j)),
            pl.BlockSpec((1, tn), lambda j: (0, j)),
        ],
        out_specs=pl.BlockSpec((rows, tn), lambda j: (0, j)),
        out_shape=jax.ShapeDtypeStruct((rows, n), F32),
        compiler_params=pltpu.CompilerParams(
            dimension_semantics=("arbitrary",), vmem_limit_bytes=VMEM_LIMIT),
        name="ada",
    )(c_pad, w_ada, b_ada)


INPROJ_TM = 512
INPROJ_CW = 256
TAIL = 8
Q_SCALE = ATT_HEADDIM ** -0.5 * LOG2E


def _inproj_kernel(x_ref, nw_ref, scale_ref, shift_ref, w_ref, ws_ref, cw_ref, cb_ref,
                   bias_ref, alog_ref, dskip_ref, ssm_nw_ref, tril_ref, ee_ref, eq_ref, ek_ref,
                   p_ref, y_ref, qa_ref, ka_ref,
                   ext_ref, z_ref, xs_ref, bc_ref, sm_ref, state_ref, carry_ref):
    tm = INPROJ_TM

    @pl.when(pl.program_id(1) == 0)
    def _():
        ext_ref[:, 0:TAIL, :] = jnp.zeros((ext_ref.shape[0], TAIL, LANES), F32)
        state_ref[...] = jnp.zeros(state_ref.shape, F32)
        carry_ref[...] = jnp.zeros(carry_ref.shape, F32)

    x = x_ref[0]
    ms = jnp.mean(x * x, axis=-1, keepdims=True)
    y = x * lax.rsqrt(ms + EPS) * nw_ref[...]
    h = y * (1.0 + scale_ref[0]) + shift_ref[0]
    h_hi, h_lo = _split2(h)
    hh = _dot(h_hi, ws_ref[...])
    sm_ref[0] = hh[:, 0:LANES] + hh[:, LANES:2 * LANES] + _dot(h_lo, ws_ref[:, 0:LANES])

    def conv_channel(col):
        if COL_XS <= col < COL_XS + SSM_INNER:
            return col - COL_XS
        if COL_BC <= col < COL_BC + BC_W:
            return SSM_INNER + col - COL_BC
        return None

    def is_ssd_side(col):
        return conv_channel(col) is not None or COL_Z_SSM <= col < COL_Z_SSM + SSM_INNER

    starts = range(0, P_COLS, INPROJ_CW)
    for col0 in [c for c in starts if is_ssd_side(c)] + [None] + [c for c in starts if not is_ssd_side(c)]:
        if col0 is None:
            _ssd_rows(z_ref, xs_ref, bc_ref, sm_ref, bias_ref, alog_ref, dskip_ref, ssm_nw_ref,
                      tril_ref, ee_ref, eq_ref, ek_ref, y_ref, qa_ref, ka_ref,
                      state_ref, carry_ref)
            continue
        acc = _dot(h_hi, w_ref[:, col0:col0 + INPROJ_CW])
        if COL_Q <= col0 < COL_Q + ATT_INNER:
            acc = acc * Q_SCALE
        if COL_Z_SSM <= col0 < COL_Z_SSM + SSM_INNER:
            z_ref[0, :, col0 - COL_Z_SSM:col0 - COL_Z_SSM + INPROJ_CW] = acc.astype(BF16)
            continue
        if conv_channel(col0) is None:
            p_ref[0, :, col0 - P_OUT0:col0 - P_OUT0 + INPROJ_CW] = acc.astype(BF16)
            continue
        for t in range(INPROJ_CW // LANES):
            ch0 = conv_channel(col0 + t * LANES)
            e = ch0 // LANES
            chs = slice(ch0, ch0 + LANES)
            u = acc[:, t * LANES:(t + 1) * LANES]
            ext_ref[e, TAIL:TAIL + tm, :] = u
            a = cb_ref[:, chs] + cw_ref[CONV_K - 1:CONV_K, chs] * u
            for k in range(CONV_K - 1):
                off = TAIL - (CONV_K - 1) + k
                a = a + cw_ref[k:k + 1, chs] * ext_ref[e, off:off + tm, :]
            if ch0 < SSM_INNER:
                xs_ref[0, :, chs] = _silu(a)
            else:
                bc_ref[0, :, ch0 - SSM_INNER:ch0 - SSM_INNER + LANES] = _silu(a)
            ext_ref[e, 0:TAIL, :] = ext_ref[e, tm:tm + TAIL, :]


def _inproj(x, norm_w, scale, shift, w_big, w_small, conv_w, conv_b,
            bias_row, alog_row, dskip_e, ssm_nw, tril, ee, eq, ek):
    b, s, d = x.shape
    tm = INPROJ_TM
    conv_dim = SSM_INNER + BC_W
    const = lambda shape: pl.BlockSpec(shape, lambda bi, i: tuple(0 for _ in shape))
    resident = lambda shape: pl.BlockSpec(shape, lambda bi, i: tuple(0 for _ in shape),
                                          pipeline_mode=pl.Buffered(1))
    rows = lambda width: pl.BlockSpec((1, tm, width), lambda bi, i: (bi, i, 0))
    return pl.pallas_call(
        _inproj_kernel,
        grid=(b, s // tm),
        in_specs=[
            rows(d),
            const((1, d)),
            pl.BlockSpec((1, 1, d), lambda bi, i: (bi, 0, 0)),
            pl.BlockSpec((1, 1, d), lambda bi, i: (bi, 0, 0)),
            resident((d, P_COLS)),
            resident((d, 2 * LANES)),
            const((CONV_K, conv_dim)), const((1, conv_dim)),
            const((1, LANES)), const((1, LANES)),
            const((1, SSM_INNER)), const((1, SSM_INNER)),
            const((CHUNK, CHUNK)), const((2 * LANES, SSM_INNER)),
            const((3 * LANES, LANES)), const((3 * LANES, LANES)),
        ],
        out_specs=[rows(P_OUT_COLS), rows(SSM_INNER), rows(LANES), rows(LANES)],
        out_shape=[
            jax.ShapeDtypeStruct((b, s, P_OUT_COLS), BF16),
            jax.ShapeDtypeStruct((b, s, SSM_INNER), BF16),
            jax.ShapeDtypeStruct((b, s, LANES), BF16),
            jax.ShapeDtypeStruct((b, s, LANES), BF16),
        ],
        scratch_shapes=[
            pltpu.VMEM((conv_dim // LANES, tm + TAIL, LANES), F32),
            pltpu.VMEM((1, tm, SSM_INNER), BF16),
            pltpu.VMEM((1, tm, SSM_INNER), F32),
            pltpu.VMEM((1, tm, BC_W), F32),
            pltpu.VMEM((1, tm, LANES), F32),
            pltpu.VMEM((SSM_GROUPS, LANES, GROUP_W), F32),
            pltpu.VMEM((1, LANES), F32),
        ],
        compiler_params=pltpu.CompilerParams(
            dimension_semantics=("arbitrary", "arbitrary"), vmem_limit_bytes=VMEM_LIMIT),
        name="inproj_ssd",
    )(x, norm_w, scale, shift, w_big, w_small, conv_w, conv_b,
      bias_row, alog_row, dskip_e, ssm_nw, tril, ee, eq, ek)


def _ssd_rows(z_ref, xs_ref, bc_ref, sm_ref,
              bias_ref, alog_ref, dskip_ref, nw_ref, tril_ref, ee_ref, eq_ref, ek_ref,
              y_ref, qa_ref, ka_ref,
              state_ref, carry_ref):
    rows = z_ref.shape[1]
    L = CHUNK

    lane = lax.broadcasted_iota(jnp.int32, (L, LANES), 1)
    is_dt = lane < SSM_HEADS
    is_f = jnp.logical_and(lane >= F_LANE0, lane < F_LANE0 + ATT_HEADS)
    row_i = lax.broadcasted_iota(jnp.int32, (L, L), 0)
    col_i = lax.broadcasted_iota(jnp.int32, (L, L), 1)
    causal = row_i >= col_i
    neg_a = -jnp.exp(alog_ref[...]) * LOG2E
    tril = tril_ref[...]

    def chunk(c):
        r0 = c * L
        v = sm_ref[0, pl.ds(r0, L), :] + bias_ref[...]
        t = jnp.log(1.0 + jnp.exp(-jnp.abs(v)))
        dt = jnp.maximum(v, 0.0) + t
        lf = jnp.minimum(v, 0.0) - t
        val = jnp.where(is_dt, dt * neg_a, jnp.where(is_f, lf, 0.0))
        v_hi, v_mid, v_lo = _split3(val)
        cs3 = _dot(tril, jnp.concatenate([v_hi, v_mid, v_lo], axis=1))
        cs = cs3[:, 0:LANES] + cs3[:, LANES:2 * LANES] + cs3[:, 2 * LANES:3 * LANES]

        fc = jnp.where(is_f, cs + carry_ref[...], 0.0)
        carry_ref[...] = fc[L - 1:L, :]
        fa = jnp.where(lane == ONE_LANE, 1.0, fc * LOG2E)
        f_hi, f_mid, f_lo = _split3(fa)
        f3 = jnp.concatenate([f_hi, f_mid, f_lo], axis=1)
        qa_ref[0, pl.ds(r0, L), :] = _dot(f3, eq_ref[...]).astype(BF16)
        ka_ref[0, pl.ds(r0, L), :] = _dot(f3, ek_ref[...]).astype(BF16)

        a_cs = jnp.where(is_dt, cs, 0.0)
        a_last = a_cs[L - 1:L, :]
        dtm = jnp.where(is_dt, dt, 0.0)
        w2 = jnp.where(is_dt, jnp.exp2(a_cs), 0.0)
        w1 = dtm * jnp.exp2(a_last - a_cs)
        st_hi, st_lo = _split2(jnp.concatenate([dtm, w2, w1], axis=0))
        ex = _dot(jnp.concatenate([st_hi, st_lo], axis=1), ee_ref[...])
        dt_e = ex[0:L]
        w2_e = ex[L:2 * L]
        w1_e = ex[2 * L:3 * L]

        xsc = xs_ref[0, pl.ds(r0, L), :].astype(F32)
        bcv = bc_ref[0, pl.ds(r0, L), :].astype(F32)
        b_all = bcv[:, 0:LANES]
        c_all = bcv[:, LANES:2 * LANES]
        b_t = b_all.T.astype(BF16)
        xd = xsc * dt_e
        xw1 = (xsc * w1_e).astype(BF16)
        a_cs_t = a_cs.T

        ys = []
        for g in range(SSM_GROUPS):
            gsl = slice(g * GROUP_W, (g + 1) * GROUP_W)
            in_g = jnp.logical_and(lane >= g * SSM_STATE, lane < (g + 1) * SSM_STATE)
            c_m = jnp.where(in_g, c_all, 0.0).astype(BF16)
            cb = _dot(c_m, b_t)
            s_g = state_ref[g]
            y_off = _dot(c_m, s_g.astype(BF16)) * w2_e[:, gsl]
            pieces = []
            for j in range(GROUP_W // LANES):
                h_a = g * (SSM_HEADS // SSM_GROUPS) + 2 * j
                xp = xd[:, g * GROUP_W + j * LANES:g * GROUP_W + (j + 1) * LANES]
                x_a = jnp.where(lane < SSM_HEADDIM, xp, 0.0).astype(BF16)
                x_b = jnp.where(lane >= SSM_HEADDIM, xp, 0.0).astype(BF16)

                def lmat(h):
                    seg = a_cs[:, h:h + 1] - a_cs_t[h:h + 1, :]
                    return (cb * jnp.exp2(jnp.where(causal, seg, -jnp.inf))).astype(BF16)

                pieces.append(_dot(lmat(h_a), x_a) + _dot(lmat(h_a + 1), x_b))
            ys.append(jnp.concatenate(pieces, axis=1) + y_off)
            state_ref[g] = s_g * w2_e[L - 1:L, gsl] + _dot(b_t, xw1[:, gsl])

        y = jnp.concatenate(ys, axis=1) + dskip_ref[...] * xsc
        z = z_ref[0, pl.ds(r0, L), :].astype(F32)
        yg = y * _silu(z)
        ms = jnp.mean(yg * yg, axis=-1, keepdims=True)
        y_ref[0, pl.ds(r0, L), :] = (yg * lax.rsqrt(ms + EPS) * nw_ref[...]).astype(BF16)

    for c in range(rows // L):
        chunk(c)


ATT_TQ = 512
ATT_TK = ATT_TQ
ATT_SPARE_SLOT = 2
ATT_HPS = 2
ATT_QSUB = 8
ATT_UNROLL = 4


def _attn_kernel(q_ref, qa_ref, k_ref, ka_ref, v_ref, z_ref, o_ref,
                 kam_ref, s_ref, mx_ref, m_ref, acc_ref):
    tk = ATT_TK

    @pl.when(pl.program_id(2) == 0)
    def _():
        lane_head = lax.broadcasted_iota(jnp.int32, (tk, LANES), 1) // AUG_W
        for hh in range(ATT_HPS):
            own = lane_head == pl.program_id(1) * ATT_HPS + hh

            def fill(t, carry):
                r0 = pl.multiple_of(t * tk, tk)
                blk = ka_ref[0, pl.ds(r0, tk), :].astype(F32)
                kam_ref[hh, pl.ds(r0, tk), :] = jnp.where(own, blk, 0.0).astype(BF16)
                return carry

            lax.fori_loop(0, ka_ref.shape[1] // tk, fill, 0)

    refs = (q_ref, qa_ref, k_ref, kam_ref, v_ref, z_ref, o_ref, s_ref, mx_ref, m_ref, acc_ref)
    _attn_first_scores(_attn_q_operand(0, q_ref, qa_ref), k_ref, kam_ref, s_ref, mx_ref)

    def sub_tile(sub, carry):
        _attn_q_tile(sub, *refs)
        return carry

    lax.fori_loop(0, ATT_QSUB, sub_tile, 0)


def _attn_q_operand(sub, q_ref, qa_ref):
    hd = ATT_HEADDIM
    rows = pl.ds(pl.multiple_of(sub * ATT_TQ, ATT_TQ), ATT_TQ)
    qa = qa_ref[0, rows, :]
    return [jnp.concatenate([q_ref[0, rows, hh * hd:(hh + 1) * hd], qa], axis=1)
            for hh in range(ATT_HPS)]


def _attn_first_scores(qq, k_ref, kam_ref, s_ref, mx_ref):
    hd = ATT_HEADDIM
    for hh in range(ATT_HPS):
        kk = jnp.concatenate([k_ref[0, 0:ATT_TK, hh * hd:(hh + 1) * hd], kam_ref[hh, 0:ATT_TK, :]],
                             axis=1)
        s = lax.dot_general(qq[hh], kk, (((1,), (1,)), ((), ())), preferred_element_type=F32)
        s_ref[0, hh] = s
        mx_ref[0, hh] = jnp.broadcast_to(jnp.max(s, axis=1, keepdims=True), (ATT_TQ, LANES))


def _attn_q_tile(sub, q_ref, qa_ref, k_ref, kam_ref, v_ref, z_ref, o_ref,
                 s_ref, mx_ref, m_ref, acc_ref):
    tq, tk, hd = ATT_TQ, ATT_TK, ATT_HEADDIM
    heads = range(ATT_HPS)
    hsl = lambda hh: slice(hh * hd, (hh + 1) * hd)
    qi = pl.program_id(2) * ATT_QSUB + sub
    qrows = pl.ds(pl.multiple_of(sub * tq, tq), tq)

    qq = _attn_q_operand(sub, q_ref, qa_ref)
    m_ref[...] = jnp.full(m_ref.shape, -jnp.inf, F32)
    acc_ref[...] = jnp.zeros(acc_ref.shape, F32)
    ones_v = jnp.ones((tk, hd), BF16)

    def scores(j, slot):
        r0 = pl.multiple_of(j * tk, tk)
        for hh in heads:
            kk = jnp.concatenate(
                [k_ref[0, pl.ds(r0, tk), hsl(hh)], kam_ref[hh, pl.ds(r0, tk), :]], axis=1)
            s = lax.dot_general(qq[hh], kk, (((1,), (1,)), ((), ())), preferred_element_type=F32)
            s_ref[slot, hh] = s
            mx_ref[slot, hh] = jnp.broadcast_to(jnp.max(s, axis=1, keepdims=True), (tq, LANES))

    def update(j, slot, masked=False):
        r0 = pl.multiple_of(j * tk, tk)
        for hh in heads:
            s = s_ref[slot, hh]
            if masked:
                row = lax.broadcasted_iota(jnp.int32, (tq, tk), 0)
                col = lax.broadcasted_iota(jnp.int32, (tq, tk), 1)
                s = jnp.where(row >= col, s, -jnp.inf)
                mx = jnp.broadcast_to(jnp.max(s, axis=1, keepdims=True), (tq, LANES))
            else:
                mx = mx_ref[slot, hh]
            m_old = m_ref[hh]
            m_new = jnp.maximum(m_old, mx)
            alpha = jnp.exp2(m_old - m_new)
            p = jnp.concatenate(
                [jnp.exp2((s[:, c * LANES:(c + 1) * LANES] - m_new).astype(BF16))
                 for c in range(tk // LANES)], axis=1)
            vv = jnp.concatenate([v_ref[0, pl.ds(r0, tk), hsl(hh)], ones_v], axis=1)
            pv = _dot(p, vv)
            acc_ref[hh, :, 0:hd] = alpha * acc_ref[hh, :, 0:hd] + pv[:, 0:hd]
            acc_ref[hh, :, hd:2 * hd] = alpha * acc_ref[hh, :, hd:2 * hd] + pv[:, hd:2 * hd]
            m_ref[hh] = m_new

    def run(j0, n, last_slot=None):
        for u in range(n):
            last = u == n - 1 and last_slot is not None
            scores(j0 + u + 1, last_slot if last else (u + 1) % 2)
            update(j0 + u, u % 2)

    def body(i, carry):
        run(ATT_UNROLL * i, ATT_UNROLL)
        return carry

    def next_first_scores():
        nxt = jnp.minimum(sub + 1, ATT_QSUB - 1)
        _attn_first_scores(_attn_q_operand(nxt, q_ref, qa_ref), k_ref, kam_ref, s_ref, mx_ref)

    lax.fori_loop(0, qi // ATT_UNROLL, body, 0)
    for r in range(ATT_UNROLL):
        @pl.when(qi % ATT_UNROLL == r)
        def _(r=r):
            if r == 0:
                update(qi, 0, masked=True)
                next_first_scores()
            else:
                run(qi - r, r, last_slot=ATT_SPARE_SLOT)
                next_first_scores()
                update(qi, ATT_SPARE_SLOT, masked=True)

    for hh in heads:
        z = z_ref[0, qrows, hsl(hh)].astype(F32)
        o = acc_ref[hh, :, 0:hd] / acc_ref[hh, :, hd:2 * hd]
        o_ref[0, qrows, hsl(hh)] = (o * _silu(z)).astype(BF16)


def _attn(p, qa, ka):
    b, s, _ = p.shape
    tq = ATT_TQ * ATT_QSUB
    w = ATT_HPS * ATT_HEADDIM
    return pl.pallas_call(
        _attn_kernel,
        grid=(b, ATT_HEADS // ATT_HPS, s // tq),
        in_specs=[
            pl.BlockSpec((1, tq, w), lambda bi, h, i: (bi, i, (COL_Q - P_OUT0) // w + h)),
            pl.BlockSpec((1, tq, LANES), lambda bi, h, i: (bi, i, 0)),
            pl.BlockSpec((1, s, w), lambda bi, h, i: (bi, 0, (COL_K - P_OUT0) // w + h)),
            pl.BlockSpec((1, s, LANES), lambda bi, h, i: (bi, 0, 0)),
            pl.BlockSpec((1, s, w), lambda bi, h, i: (bi, 0, (COL_V - P_OUT0) // w + h)),
            pl.BlockSpec((1, tq, w), lambda bi, h, i: (bi, i, (COL_Z_ATT - P_OUT0) // w + h)),
        ],
        out_specs=pl.BlockSpec((1, tq, w), lambda bi, h, i: (bi, i, h)),
        out_shape=jax.ShapeDtypeStruct((b, s, ATT_INNER), BF16),
        scratch_shapes=[
            pltpu.VMEM((ATT_HPS, s, LANES), BF16),
            pltpu.VMEM((3, ATT_HPS, ATT_TQ, ATT_TK), F32),
            pltpu.VMEM((3, ATT_HPS, ATT_TQ, LANES), F32),
            pltpu.VMEM((ATT_HPS, ATT_TQ, LANES), F32),
            pltpu.VMEM((ATT_HPS, ATT_TQ, 2 * ATT_HEADDIM), F32),
        ],
        compiler_params=pltpu.CompilerParams(
            dimension_semantics=("arbitrary", "arbitrary", "arbitrary"),
            vmem_limit_bytes=VMEM_LIMIT),
        name="fox_attn",
    )(p, qa, p, ka, p, p)


OUT_TM = 1024


def _out_kernel(ys_ref, ya_ref, g_ref, x_ref, gate_ref, bg_ref, wps_ref, wpa_ref, wo_ref,
                fnw_ref, o_ref):
    g = _sigmoid(g_ref[0].astype(F32) + bg_ref[...])
    ps = _dot(ys_ref[0], wps_ref[...])
    pa = _dot(ya_ref[0], wpa_ref[...])
    merged = g[:, 0:D_MODEL] * ps + g[:, D_MODEL:2 * D_MODEL] * pa
    out = _dot(merged.astype(BF16), wo_ref[...])
    xn = x_ref[0] + gate_ref[0] * out
    ms = jnp.mean(xn * xn, axis=-1, keepdims=True)
    o_ref[0] = xn * lax.rsqrt(ms + EPS) * fnw_ref[...]


def _outproj(y_ssm, y_att, p, x, gate, b_gate, wps, wpa, wo, fnw):
    b, s, d = x.shape
    tm = OUT_TM
    const = lambda shape: pl.BlockSpec(shape, lambda bi, i: tuple(0 for _ in shape))
    return pl.pallas_call(
        _out_kernel,
        grid=(b, s // tm),
        in_specs=[
            pl.BlockSpec((1, tm, d), lambda bi, i: (bi, i, 0)),
            pl.BlockSpec((1, tm, d), lambda bi, i: (bi, i, 0)),
            pl.BlockSpec((1, tm, 2 * d), lambda bi, i: (bi, i, (COL_G - P_OUT0) // (2 * d))),
            pl.BlockSpec((1, tm, d), lambda bi, i: (bi, i, 0)),
            pl.BlockSpec((1, 1, d), lambda bi, i: (bi, 0, 0)),
            const((1, 2 * d)), const((d, d)), const((d, d)), const((d, d)), const((1, d)),
        ],
        out_specs=pl.BlockSpec((1, tm, d), lambda bi, i: (bi, i, 0)),
        out_shape=jax.ShapeDtypeStruct((b, s, d), F32),
        compiler_params=pltpu.CompilerParams(
            dimension_semantics=("arbitrary", "arbitrary"), vmem_limit_bytes=VMEM_LIMIT),
        name="outproj",
    )(y_ssm, y_att, p, x, gate, b_gate, wps, wpa, wo, fnw)


def _constants():
    tril = np.tril(np.ones((CHUNK, CHUNK), np.float32))
    ee = np.zeros((2 * LANES, SSM_INNER), np.float32)
    for h in range(SSM_HEADS):
        ee[h, h * SSM_HEADDIM:(h + 1) * SSM_HEADDIM] = 1.0
        ee[LANES + h, h * SSM_HEADDIM:(h + 1) * SSM_HEADDIM] = 1.0
    eq = np.zeros((3 * LANES, LANES), np.float32)
    ek = np.zeros((3 * LANES, LANES), np.float32)
    for h in range(ATT_HEADS):
        base = h * AUG_W
        for part in range(3):
            eq[part * LANES + F_LANE0 + h, base + 3 + part] = 1.0
            ek[part * LANES + F_LANE0 + h, base + part] = -1.0
            eq[ONE_LANE, base + part] = 1.0
            ek[ONE_LANE, base + 3 + part] = 1.0
    as_bf16 = lambda a: jnp.asarray(a, dtype=BF16)
    return as_bf16(tril), as_bf16(ee), as_bf16(eq), as_bf16(ek)


def kernel(x, c, w_ada, b_ada, norm_w, w_in, conv_w, conv_b, dt_bias, a_log, d_skip,
           ssm_norm_w, b_f, b_gate, w_proj_ssm, w_proj_att, w_out, final_norm_w):
    b, s, d = x.shape
    row = lambda v: v.reshape(1, -1).astype(F32)

    c_pad = jnp.zeros((8, d), F32).at[0:b].set(c)
    ada = _ada(c_pad, w_ada, row(b_ada))[0:b]
    shift = ada[:, 0:d].reshape(b, 1, d)
    scale = ada[:, d:2 * d].reshape(b, 1, d)
    gate = ada[:, 2 * d:3 * d].reshape(b, 1, d)

    o_z, o_xbc, o_dt, o_q, o_k, o_v, o_za, o_f, o_g = np.cumsum(
        [0, SSM_INNER, SSM_INNER + BC_W, SSM_HEADS, ATT_INNER, ATT_INNER, ATT_INNER,
         ATT_INNER, ATT_HEADS]).tolist()
    w_bf = w_in.astype(BF16)
    w_big = jnp.concatenate([
        w_bf[:, o_z:o_z + SSM_INNER],
        w_bf[:, o_xbc:o_xbc + SSM_INNER],
        w_bf[:, o_q:o_q + ATT_INNER],
        w_bf[:, o_k:o_k + ATT_INNER],
        w_bf[:, o_v:o_v + ATT_INNER],
        w_bf[:, o_za:o_za + ATT_INNER],
        w_bf[:, o_g:o_g + 2 * d],
        w_bf[:, o_xbc + SSM_INNER:o_xbc + SSM_INNER + BC_W],
    ], axis=1)
    w_sm = jnp.concatenate([
        w_in[:, o_dt:o_dt + SSM_HEADS], w_in[:, o_f:o_f + ATT_HEADS],
        jnp.zeros((d, LANES - SSM_HEADS - ATT_HEADS), F32)], axis=1)
    w_sm_hi = w_sm.astype(BF16)
    w_sm_lo = (w_sm - w_sm_hi.astype(F32)).astype(BF16)
    w_small = jnp.concatenate([w_sm_hi, w_sm_lo], axis=1)

    tril, ee, eq, ek = _constants()
    pad_lanes = lambda v, lane0: jnp.zeros((1, LANES), F32).at[0, lane0:lane0 + v.shape[0]].set(v)
    bias_row = pad_lanes(dt_bias.astype(F32), DT_LANE0) + pad_lanes(b_f.astype(F32), F_LANE0)
    alog_row = pad_lanes(a_log.astype(F32), DT_LANE0)
    dskip_e = jnp.repeat(d_skip.astype(F32), SSM_HEADDIM).reshape(1, SSM_INNER)
    p, y_ssm, qa, ka = _inproj(
        x, row(norm_w), scale, shift, w_big, w_small, conv_w.astype(F32), row(conv_b),
        bias_row, alog_row, dskip_e, row(ssm_norm_w), tril, ee, eq, ek)

    y_att = _attn(p, qa, ka)

    return _outproj(y_ssm, y_att, p, x, gate, row(b_gate),
                    w_proj_ssm.astype(BF16), w_proj_att.astype(BF16), w_out.astype(BF16),
                    row(final_norm_w))
```

```python
import functools

import jax
import jax.numpy as jnp
import numpy as np
from jax import lax
from jax.experimental import pallas as pl
from jax.experimental.pallas import tpu as pltpu

F32 = jnp.float32
BF16 = jnp.bfloat16

D_MODEL = 1024
SSM_HEADDIM = 64
SSM_HEADS = 16
SSM_INNER = SSM_HEADS * SSM_HEADDIM
SSM_GROUPS = 2
SSM_STATE = 64
CONV_K = 4
CHUNK = 128
ATT_HEADS = 8
ATT_HEADDIM = 128
ATT_INNER = ATT_HEADS * ATT_HEADDIM
EPS = 1e-6
LOG2E = 1.4426950408889634

LANES = 128
GROUP_W = SSM_INNER // SSM_GROUPS
BC_W = 2 * SSM_GROUPS * SSM_STATE

COL_Z_SSM = 0
COL_XS = 1024
COL_Q = 2048
COL_K = 3072
COL_V = 4096
COL_Z_ATT = 5120
COL_G = 6144
COL_BC = 8192
P_COLS = 8448
P_OUT0 = COL_Q
P_OUT_COLS = COL_BC - COL_Q

DT_LANE0 = 0
F_LANE0 = SSM_HEADS
ONE_LANE = LANES - 1
AUG_W = LANES // ATT_HEADS

VMEM_LIMIT = 56 * 1024 * 1024


def _dot(a, b):
    return jnp.dot(a, b, preferred_element_type=F32)


def _split2(x):
    hi = x.astype(BF16)
    lo = (x - hi.astype(F32)).astype(BF16)
    return hi, lo


def _split3(x):
    hi = x.astype(BF16)
    r1 = x - hi.astype(F32)
    mid = r1.astype(BF16)
    lo = (r1 - mid.astype(F32)).astype(BF16)
    return hi, mid, lo


def _sigmoid(x):
    return 0.5 * jnp.tanh(0.5 * x) + 0.5


def _silu(x):
    h = 0.5 * x
    return h + h * jnp.tanh(h)


def _ada_kernel(c_ref, w_ref, b_ref, o_ref):
    c = c_ref[...]
    cs = c * jax.nn.sigmoid(c)
    w = w_ref[...]
    c_hi, c_lo = _split2(cs)
    w_hi, w_lo = _split2(w)
    o_ref[...] = _dot(c_hi, w_hi) + _dot(c_lo, w_hi) + _dot(c_hi, w_lo) + b_ref[...]


def _ada(c_pad, w_ada, b_ada):
    rows = c_pad.shape[0]
    n = w_ada.shape[1]
    tn = 1024
    return pl.pallas_call(
        _ada_kernel,
        grid=(n // tn,),
        in_specs=[
            pl.BlockSpec((rows, D_MODEL), lambda j: (0, 0)),
            pl.BlockSpec((D_MODEL, tn), lambda j: (0, j)),
            pl.BlockSpec((1, tn), lambda j: (0, j)),
        ],
        out_specs=pl.BlockSpec((rows, tn), lambda j: (0, j)),
        out_shape=jax.ShapeDtypeStruct((rows, n), F32),
        compiler_params=pltpu.CompilerParams(
            dimension_semantics=("arbitrary",), vmem_limit_bytes=VMEM_LIMIT),
        name="ada",
    )(c_pad, w_ada, b_ada)


INPROJ_TM = 512
INPROJ_CW = 256
TAIL = 8
Q_SCALE = ATT_HEADDIM ** -0.5 * LOG2E


def _inproj_kernel(x_ref, nw_ref, scale_ref, shift_ref, w_ref, ws_ref, cw_ref, cb_ref,
                   bias_ref, alog_ref, dskip_ref, ssm_nw_ref, tril_ref, ee_ref, eqk_ref,
                   p_ref, y_ref, qa_ref, ka_ref,
                   ext_ref, z_ref, xs_ref, bc_ref, sm_ref, state_ref, carry_ref):
    tm = INPROJ_TM

    @pl.when(pl.program_id(1) == 0)
    def _():
        ext_ref[:, 0:TAIL, :] = jnp.zeros((ext_ref.shape[0], TAIL, LANES), F32)
        state_ref[...] = jnp.zeros(state_ref.shape, F32)
        carry_ref[...] = jnp.zeros(carry_ref.shape, F32)

    x = x_ref[0]
    ms = jnp.mean(x * x, axis=-1, keepdims=True)
    y = x * lax.rsqrt(ms + EPS) * nw_ref[...]
    h = y * (1.0 + scale_ref[0]) + shift_ref[0]
    h_hi, h_lo = _split2(h)
    hh = _dot(h_hi, ws_ref[...])
    sm_ref[0] = hh[:, 0:LANES] + hh[:, LANES:2 * LANES] + _dot(h_lo, ws_ref[:, 0:LANES])

    def conv_channel(col):
        if COL_XS <= col < COL_XS + SSM_INNER:
            return col - COL_XS
        if COL_BC <= col < COL_BC + BC_W:
            return SSM_INNER + col - COL_BC
        return None

    def is_ssd_side(col):
        return conv_channel(col) is not None or COL_Z_SSM <= col < COL_Z_SSM + SSM_INNER

    starts = range(0, P_COLS, INPROJ_CW)
    for col0 in [c for c in starts if is_ssd_side(c)] + [None] + [c for c in starts if not is_ssd_side(c)]:
        if col0 is None:
            _ssd_rows(z_ref, xs_ref, bc_ref, sm_ref, bias_ref, alog_ref, dskip_ref, ssm_nw_ref,
                      tril_ref, ee_ref, eqk_ref, y_ref, qa_ref, ka_ref,
                      state_ref, carry_ref)
            continue
        acc = _dot(h_hi, w_ref[:, col0:col0 + INPROJ_CW])
        if COL_Q <= col0 < COL_Q + ATT_INNER:
            acc = acc * Q_SCALE
        if COL_Z_SSM <= col0 < COL_Z_SSM + SSM_INNER:
            z_ref[0, :, col0 - COL_Z_SSM:col0 - COL_Z_SSM + INPROJ_CW] = acc.astype(BF16)
            continue
        if conv_channel(col0) is None:
            p_ref[0, :, col0 - P_OUT0:col0 - P_OUT0 + INPROJ_CW] = acc.astype(BF16)
            continue
        for t in range(INPROJ_CW // LANES):
            ch0 = conv_channel(col0 + t * LANES)
            e = ch0 // LANES
            chs = slice(ch0, ch0 + LANES)
            u = acc[:, t * LANES:(t + 1) * LANES]
            ext_ref[e, TAIL:TAIL + tm, :] = u
            a = cb_ref[:, chs] + cw_ref[CONV_K - 1:CONV_K, chs] * u
            for k in range(CONV_K - 1):
                off = TAIL - (CONV_K - 1) + k
                a = a + cw_ref[k:k + 1, chs] * ext_ref[e, off:off + tm, :]
            act = a + a * jnp.tanh(a)
            if ch0 < SSM_INNER:
                xs_ref[0, :, chs] = act
            else:
                bc_ref[0, :, ch0 - SSM_INNER:ch0 - SSM_INNER + LANES] = act
            ext_ref[e, 0:TAIL, :] = ext_ref[e, tm:tm + TAIL, :]


def _inproj(x, norm_w, scale, shift, w_big, w_small, conv_w, conv_b,
            bias_row, alog_row, dskip_e, ssm_nw, tril, ee, eqk):
    b, s, d = x.shape
    tm = INPROJ_TM
    conv_dim = SSM_INNER + BC_W
    const = lambda shape: pl.BlockSpec(shape, lambda bi, i: tuple(0 for _ in shape))
    resident = lambda shape: pl.BlockSpec(shape, lambda bi, i: tuple(0 for _ in shape),
                                          pipeline_mode=pl.Buffered(1))
    rows = lambda width: pl.BlockSpec((1, tm, width), lambda bi, i: (bi, i, 0))
    return pl.pallas_call(
        _inproj_kernel,
        grid=(b, s // tm),
        in_specs=[
            rows(d),
            const((1, d)),
            pl.BlockSpec((1, 1, d), lambda bi, i: (bi, 0, 0)),
            pl.BlockSpec((1, 1, d), lambda bi, i: (bi, 0, 0)),
            resident((d, P_COLS)),
            resident((d, 2 * LANES)),
            const((CONV_K, conv_dim)), const((1, conv_dim)),
            const((1, LANES)), const((1, LANES)),
            const((1, SSM_INNER)), const((1, SSM_INNER)),
            const((CHUNK, CHUNK)), const((2 * LANES, SSM_INNER)),
            const((3 * LANES, 2 * LANES)),
        ],
        out_specs=[rows(P_OUT_COLS), rows(SSM_INNER), rows(LANES), rows(LANES)],
        out_shape=[
            jax.ShapeDtypeStruct((b, s, P_OUT_COLS), BF16),
            jax.ShapeDtypeStruct((b, s, SSM_INNER), BF16),
            jax.ShapeDtypeStruct((b, s, LANES), BF16),
            jax.ShapeDtypeStruct((b, s, LANES), BF16),
        ],
        scratch_shapes=[
            pltpu.VMEM((conv_dim // LANES, tm + TAIL, LANES), F32),
            pltpu.VMEM((1, tm, SSM_INNER), BF16),
            pltpu.VMEM((1, tm, SSM_INNER), F32),
            pltpu.VMEM((1, tm, BC_W), F32),
            pltpu.VMEM((1, tm, LANES), F32),
            pltpu.VMEM((SSM_GROUPS, LANES, GROUP_W), F32),
            pltpu.VMEM((1, LANES), F32),
        ],
        compiler_params=pltpu.CompilerParams(
            dimension_semantics=("arbitrary", "arbitrary"), vmem_limit_bytes=VMEM_LIMIT),
        name="inproj_ssd",
    )(x, norm_w, scale, shift, w_big, w_small, conv_w, conv_b,
      bias_row, alog_row, dskip_e, ssm_nw, tril, ee, eqk)


def _ssd_rows(z_ref, xs_ref, bc_ref, sm_ref,
              bias_ref, alog_ref, dskip_ref, nw_ref, tril_ref, ee_ref, eqk_ref,
              y_ref, qa_ref, ka_ref,
              state_ref, carry_ref):
    rows = z_ref.shape[1]
    L = CHUNK

    lane = lax.broadcasted_iota(jnp.int32, (L, LANES), 1)
    is_dt = lane < SSM_HEADS
    is_f = jnp.logical_and(lane >= F_LANE0, lane < F_LANE0 + ATT_HEADS)
    row_i = lax.broadcasted_iota(jnp.int32, (L, L), 0)
    col_i = lax.broadcasted_iota(jnp.int32, (L, L), 1)
    causal = row_i >= col_i
    neg_a = -jnp.exp(alog_ref[...]) * LOG2E
    tril = tril_ref[...]

    def chunk(c):
        r0 = c * L
        v = sm_ref[0, pl.ds(r0, L), :] + bias_ref[...]
        t = jnp.log(1.0 + jnp.exp(-jnp.abs(v)))
        dt = jnp.maximum(v, 0.0) + t
        lf = jnp.minimum(v, 0.0) - t
        val = jnp.where(is_dt, dt * neg_a, jnp.where(is_f, lf, 0.0))
        v_hi, v_mid, v_lo = _split3(val)
        cs3 = _dot(tril, jnp.concatenate([v_hi, v_mid, v_lo], axis=1))
        cs = cs3[:, 0:LANES] + cs3[:, LANES:2 * LANES] + cs3[:, 2 * LANES:3 * LANES]

        fc = jnp.where(is_f, cs + carry_ref[...], 0.0)
        carry_ref[...] = fc[L - 1:L, :]
        fa = jnp.where(lane == ONE_LANE, 1.0, fc * LOG2E)
        f_hi, f_mid, f_lo = _split3(fa)
        f3 = jnp.concatenate([f_hi, f_mid, f_lo], axis=1)
        qk = _dot(f3, eqk_ref[...]).astype(BF16)
        qa_ref[0, pl.ds(r0, L), :] = qk[:, 0:LANES]
        ka_ref[0, pl.ds(r0, L), :] = qk[:, LANES:2 * LANES]

        a_cs = jnp.where(is_dt, cs, 0.0)
        a_last = a_cs[L - 1:L, :]
        dtm = jnp.where(is_dt, dt, 0.0)
        w2 = jnp.where(is_dt, jnp.exp2(a_cs), 0.0)
        w1 = dtm * jnp.exp2(a_last - a_cs)
        st_hi, st_lo = _split2(jnp.concatenate([dtm, w2, w1], axis=0))
        ex = _dot(jnp.concatenate([st_hi, st_lo], axis=1), ee_ref[...])
        dt_e = ex[0:L]
        w2_e = ex[L:2 * L]
        w1_e = ex[2 * L:3 * L]

        xsc = xs_ref[0, pl.ds(r0, L), :].astype(F32)
        bcv = bc_ref[0, pl.ds(r0, L), :].astype(F32)
        b_all = bcv[:, 0:LANES]
        c_all = bcv[:, LANES:2 * LANES]
        b_t = b_all.T.astype(BF16)
        xd = xsc * dt_e
        xw1 = (xsc * w1_e).astype(BF16)
        a_cs_t = a_cs.T

        ys = []
        for g in range(SSM_GROUPS):
            gsl = slice(g * GROUP_W, (g + 1) * GROUP_W)
            in_g = jnp.logical_and(lane >= g * SSM_STATE, lane < (g + 1) * SSM_STATE)
            c_m = jnp.where(in_g, c_all, 0.0).astype(BF16)
            cb = jnp.where(causal, _dot(c_m, b_t), 0.0).astype(BF16)
            s_g = state_ref[g]
            y_off = _dot(c_m, s_g.astype(BF16)) * w2_e[:, gsl]
            pieces = []
            for j in range(GROUP_W // LANES):
                h_a = g * (SSM_HEADS // SSM_GROUPS) + 2 * j
                xp = xd[:, g * GROUP_W + j * LANES:g * GROUP_W + (j + 1) * LANES]
                x_a = jnp.where(lane < SSM_HEADDIM, xp, 0.0).astype(BF16)
                x_b = jnp.where(lane >= SSM_HEADDIM, xp, 0.0).astype(BF16)

                def lmat(h):
                    seg = (a_cs[:, h:h + 1] - a_cs_t[h:h + 1, :]).astype(BF16)
                    return cb * jnp.exp2(jnp.minimum(seg, 0))

                pieces.append(_dot(jnp.concatenate([lmat(h_a), lmat(h_a + 1)], axis=1),
                                   jnp.concatenate([x_a, x_b], axis=0)))
            ys.append(jnp.concatenate(pieces, axis=1) + y_off)
            state_ref[g] = s_g * w2_e[L - 1:L, gsl] + _dot(b_t, xw1[:, gsl])

        y = jnp.concatenate(ys, axis=1) + dskip_ref[...] * xsc
        z = z_ref[0, pl.ds(r0, L), :].astype(F32)
        yg = y * _silu(z)
        ms = jnp.mean(yg * yg, axis=-1, keepdims=True)
        y_ref[0, pl.ds(r0, L), :] = (yg * lax.rsqrt(ms + EPS) * nw_ref[...]).astype(BF16)

    for c in range(rows // L):
        chunk(c)


ATT_TQ = 512
ATT_TK = ATT_TQ
ATT_SPARE_SLOT = 2
ATT_HPS = 2
ATT_QSUB = 8
ATT_UNROLL = 4


def _attn_kernel(q_ref, qa_ref, k_ref, ka_ref, v_ref, z_ref, o_ref,
                 kam_ref, s_ref, mx_ref, m_ref, acc_ref):
    tk = ATT_TK

    @pl.when(pl.program_id(2) == 0)
    def _():
        lane_head = lax.broadcasted_iota(jnp.int32, (tk, LANES), 1) // AUG_W
        for hh in range(ATT_HPS):
            own = lane_head == pl.program_id(1) * ATT_HPS + hh

            def fill(t, carry):
                r0 = pl.multiple_of(t * tk, tk)
                blk = ka_ref[0, pl.ds(r0, tk), :].astype(F32)
                kam_ref[hh, pl.ds(r0, tk), :] = jnp.where(own, blk, 0.0).astype(BF16)
                return carry

            lax.fori_loop(0, ka_ref.shape[1] // tk, fill, 0)

    refs = (q_ref, qa_ref, k_ref, kam_ref, v_ref, z_ref, o_ref, s_ref, mx_ref, m_ref, acc_ref)
    _attn_first_scores(_attn_q_operand(0, q_ref, qa_ref), k_ref, kam_ref, s_ref, mx_ref)

    def sub_tile(sub, carry):
        _attn_q_tile(sub, *refs)
        return carry

    lax.fori_loop(0, ATT_QSUB, sub_tile, 0)


def _attn_q_operand(sub, q_ref, qa_ref):
    hd = ATT_HEADDIM
    rows = pl.ds(pl.multiple_of(sub * ATT_TQ, ATT_TQ), ATT_TQ)
    qa = qa_ref[0, rows, :]
    return [jnp.concatenate([q_ref[0, rows, hh * hd:(hh + 1) * hd], qa], axis=1)
            for hh in range(ATT_HPS)]


def _attn_first_scores(qq, k_ref, kam_ref, s_ref, mx_ref):
    hd = ATT_HEADDIM
    for hh in range(ATT_HPS):
        kk = jnp.concatenate([k_ref[0, 0:ATT_TK, hh * hd:(hh + 1) * hd], kam_ref[hh, 0:ATT_TK, :]],
                             axis=1)
        s = lax.dot_general(qq[hh], kk, (((1,), (1,)), ((), ())), preferred_element_type=F32)
        s_ref[0, hh] = s
        mx_ref[0, hh] = jnp.broadcast_to(jnp.max(s, axis=1, keepdims=True), (ATT_TQ, LANES))


def _attn_q_tile(sub, q_ref, qa_ref, k_ref, kam_ref, v_ref, z_ref, o_ref,
                 s_ref, mx_ref, m_ref, acc_ref):
    tq, tk, hd = ATT_TQ, ATT_TK, ATT_HEADDIM
    heads = range(ATT_HPS)
    hsl = lambda hh: slice(hh * hd, (hh + 1) * hd)
    qi = pl.program_id(2) * ATT_QSUB + sub
    qrows = pl.ds(pl.multiple_of(sub * tq, tq), tq)

    qq = _attn_q_operand(sub, q_ref, qa_ref)
    m_ref[...] = jnp.full(m_ref.shape, -jnp.inf, F32)
    acc_ref[...] = jnp.zeros(acc_ref.shape, F32)
    ones_v = jnp.ones((tk, hd), BF16)

    def scores(j, slot):
        r0 = pl.multiple_of(j * tk, tk)
        for hh in heads:
            kk = jnp.concatenate(
                [k_ref[0, pl.ds(r0, tk), hsl(hh)], kam_ref[hh, pl.ds(r0, tk), :]], axis=1)
            s = lax.dot_general(qq[hh], kk, (((1,), (1,)), ((), ())), preferred_element_type=F32)
            s_ref[slot, hh] = s
            mx_ref[slot, hh] = jnp.broadcast_to(jnp.max(s, axis=1, keepdims=True), (tq, LANES))

    def update(j, slot, masked=False):
        r0 = pl.multiple_of(j * tk, tk)
        for hh in heads:
            s = s_ref[slot, hh]
            if masked:
                row = lax.broadcasted_iota(jnp.int32, (tq, tk), 0)
                col = lax.broadcasted_iota(jnp.int32, (tq, tk), 1)
                s = jnp.where(row >= col, s, -jnp.inf)
                mx = jnp.broadcast_to(jnp.max(s, axis=1, keepdims=True), (tq, LANES))
            else:
                mx = mx_ref[slot, hh]
            m_old = m_ref[hh]
            m_new = jnp.maximum(m_old, mx)
            alpha = jnp.exp2(m_old - m_new)
            p = jnp.concatenate(
                [jnp.exp2((s[:, c * LANES:(c + 1) * LANES] - m_new).astype(BF16))
                 for c in range(tk // LANES)], axis=1)
            vv = jnp.concatenate([v_ref[0, pl.ds(r0, tk), hsl(hh)], ones_v], axis=1)
            pv = _dot(p, vv)
            acc_ref[hh, :, 0:hd] = alpha * acc_ref[hh, :, 0:hd] + pv[:, 0:hd]
            acc_ref[hh, :, hd:2 * hd] = alpha * acc_ref[hh, :, hd:2 * hd] + pv[:, hd:2 * hd]
            m_ref[hh] = m_new

    def run(j0, n, last_slot=None):
        for u in range(n):
            last = u == n - 1 and last_slot is not None
            scores(j0 + u + 1, last_slot if last else (u + 1) % 2)
            update(j0 + u, u % 2)

    def body(i, carry):
        run(ATT_UNROLL * i, ATT_UNROLL)
        return carry

    def next_first_scores():
        nxt = jnp.minimum(sub + 1, ATT_QSUB - 1)
        _attn_first_scores(_attn_q_operand(nxt, q_ref, qa_ref), k_ref, kam_ref, s_ref, mx_ref)

    lax.fori_loop(0, qi // ATT_UNROLL, body, 0)
    for r in range(ATT_UNROLL):
        @pl.when(qi % ATT_UNROLL == r)
        def _(r=r):
            if r == 0:
                update(qi, 0, masked=True)
                next_first_scores()
            else:
                run(qi - r, r, last_slot=ATT_SPARE_SLOT)
                next_first_scores()
                update(qi, ATT_SPARE_SLOT, masked=True)

    for hh in heads:
        z = z_ref[0, qrows, hsl(hh)].astype(F32)
        o = acc_ref[hh, :, 0:hd] / acc_ref[hh, :, hd:2 * hd]
        o_ref[0, qrows, hsl(hh)] = (o * _silu(z)).astype(BF16)


def _attn(p, qa, ka):
    b, s, _ = p.shape
    tq = ATT_TQ * ATT_QSUB
    w = ATT_HPS * ATT_HEADDIM
    return pl.pallas_call(
        _attn_kernel,
        grid=(b, ATT_HEADS // ATT_HPS, s // tq),
        in_specs=[
            pl.BlockSpec((1, tq, w), lambda bi, h, i: (bi, i, (COL_Q - P_OUT0) // w + h)),
            pl.BlockSpec((1, tq, LANES), lambda bi, h, i: (bi, i, 0)),
            pl.BlockSpec((1, s, w), lambda bi, h, i: (bi, 0, (COL_K - P_OUT0) // w + h)),
            pl.BlockSpec((1, s, LANES), lambda bi, h, i: (bi, 0, 0)),
            pl.BlockSpec((1, s, w), lambda bi, h, i: (bi, 0, (COL_V - P_OUT0) // w + h)),
            pl.BlockSpec((1, tq, w), lambda bi, h, i: (bi, i, (COL_Z_ATT - P_OUT0) // w + h)),
        ],
        out_specs=pl.BlockSpec((1, tq, w), lambda bi, h, i: (bi, i, h)),
        out_shape=jax.ShapeDtypeStruct((b, s, ATT_INNER), BF16),
        scratch_shapes=[
            pltpu.VMEM((ATT_HPS, s, LANES), BF16),
            pltpu.VMEM((3, ATT_HPS, ATT_TQ, ATT_TK), F32),
            pltpu.VMEM((3, ATT_HPS, ATT_TQ, LANES), F32),
            pltpu.VMEM((ATT_HPS, ATT_TQ, LANES), F32),
            pltpu.VMEM((ATT_HPS, ATT_TQ, 2 * ATT_HEADDIM), F32),
        ],
        compiler_params=pltpu.CompilerParams(
            dimension_semantics=("arbitrary", "arbitrary", "arbitrary"),
            vmem_limit_bytes=VMEM_LIMIT),
        name="fox_attn",
    )(p, qa, p, ka, p, p)


OUT_TM = 1024


def _out_kernel(ys_ref, ya_ref, g_ref, x_ref, gate_ref, bg_ref, wps_ref, wpa_ref, wo_ref,
                fnw_ref, o_ref):
    g = _sigmoid(g_ref[0].astype(F32) + bg_ref[...])
    ps = _dot(ys_ref[0], wps_ref[...])
    pa = _dot(ya_ref[0], wpa_ref[...])
    merged = g[:, 0:D_MODEL] * ps + g[:, D_MODEL:2 * D_MODEL] * pa
    out = _dot(merged.astype(BF16), wo_ref[...])
    xn = x_ref[0] + gate_ref[0] * out
    ms = jnp.mean(xn * xn, axis=-1, keepdims=True)
    o_ref[0] = xn * lax.rsqrt(ms + EPS) * fnw_ref[...]


def _outproj(y_ssm, y_att, p, x, gate, b_gate, wps, wpa, wo, fnw):
    b, s, d = x.shape
    tm = OUT_TM
    const = lambda shape: pl.BlockSpec(shape, lambda bi, i: tuple(0 for _ in shape))
    return pl.pallas_call(
        _out_kernel,
        grid=(b, s // tm),
        in_specs=[
            pl.BlockSpec((1, tm, d), lambda bi, i: (bi, i, 0)),
            pl.BlockSpec((1, tm, d), lambda bi, i: (bi, i, 0)),
            pl.BlockSpec((1, tm, 2 * d), lambda bi, i: (bi, i, (COL_G - P_OUT0) // (2 * d))),
            pl.BlockSpec((1, tm, d), lambda bi, i: (bi, i, 0)),
            pl.BlockSpec((1, 1, d), lambda bi, i: (bi, 0, 0)),
            const((1, 2 * d)), const((d, d)), const((d, d)), const((d, d)), const((1, d)),
        ],
        out_specs=pl.BlockSpec((1, tm, d), lambda bi, i: (bi, i, 0)),
        out_shape=jax.ShapeDtypeStruct((b, s, d), F32),
        compiler_params=pltpu.CompilerParams(
            dimension_semantics=("arbitrary", "arbitrary"), vmem_limit_bytes=VMEM_LIMIT),
        name="outproj",
    )(y_ssm, y_att, p, x, gate, b_gate, wps, wpa, wo, fnw)


def _constants():
    tril = np.tril(np.ones((CHUNK, CHUNK), np.float32))
    ee = np.zeros((2 * LANES, SSM_INNER), np.float32)
    for h in range(SSM_HEADS):
        ee[h, h * SSM_HEADDIM:(h + 1) * SSM_HEADDIM] = 1.0
        ee[LANES + h, h * SSM_HEADDIM:(h + 1) * SSM_HEADDIM] = 1.0
    eq = np.zeros((3 * LANES, LANES), np.float32)
    ek = np.zeros((3 * LANES, LANES), np.float32)
    for h in range(ATT_HEADS):
        base = h * AUG_W
        for part in range(3):
            eq[part * LANES + F_LANE0 + h, base + 3 + part] = 1.0
            ek[part * LANES + F_LANE0 + h, base + part] = -1.0
            eq[ONE_LANE, base + part] = 1.0
            ek[ONE_LANE, base + 3 + part] = 1.0
    as_bf16 = lambda a: jnp.asarray(a, dtype=BF16)
    return as_bf16(tril), as_bf16(ee), as_bf16(np.concatenate([eq, ek], axis=1))


def kernel(x, c, w_ada, b_ada, norm_w, w_in, conv_w, conv_b, dt_bias, a_log, d_skip,
           ssm_norm_w, b_f, b_gate, w_proj_ssm, w_proj_att, w_out, final_norm_w):
    b, s, d = x.shape
    row = lambda v: v.reshape(1, -1).astype(F32)

    c_pad = jnp.zeros((8, d), F32).at[0:b].set(c)
    ada = _ada(c_pad, w_ada, row(b_ada))[0:b]
    shift = ada[:, 0:d].reshape(b, 1, d)
    scale = ada[:, d:2 * d].reshape(b, 1, d)
    gate = ada[:, 2 * d:3 * d].reshape(b, 1, d)

    o_z, o_xbc, o_dt, o_q, o_k, o_v, o_za, o_f, o_g = np.cumsum(
        [0, SSM_INNER, SSM_INNER + BC_W, SSM_HEADS, ATT_INNER, ATT_INNER, ATT_INNER,
         ATT_INNER, ATT_HEADS]).tolist()
    w_bf = w_in.astype(BF16)
    w_big = jnp.concatenate([
        w_bf[:, o_z:o_z + SSM_INNER],
        w_bf[:, o_xbc:o_xbc + SSM_INNER],
        w_bf[:, o_q:o_q + ATT_INNER],
        w_bf[:, o_k:o_k + ATT_INNER],
        w_bf[:, o_v:o_v + ATT_INNER],
        w_bf[:, o_za:o_za + ATT_INNER],
        w_bf[:, o_g:o_g + 2 * d],
        w_bf[:, o_xbc + SSM_INNER:o_xbc + SSM_INNER + BC_W],
    ], axis=1)
    w_sm = jnp.concatenate([
        w_in[:, o_dt:o_dt + SSM_HEADS], w_in[:, o_f:o_f + ATT_HEADS],
        jnp.zeros((d, LANES - SSM_HEADS - ATT_HEADS), F32)], axis=1)
    w_sm_hi = w_sm.astype(BF16)
    w_sm_lo = (w_sm - w_sm_hi.astype(F32)).astype(BF16)
    w_small = jnp.concatenate([w_sm_hi, w_sm_lo], axis=1)

    tril, ee, eqk = _constants()
    pad_lanes = lambda v, lane0: jnp.zeros((1, LANES), F32).at[0, lane0:lane0 + v.shape[0]].set(v)
    bias_row = pad_lanes(dt_bias.astype(F32), DT_LANE0) + pad_lanes(b_f.astype(F32), F_LANE0)
    alog_row = pad_lanes(a_log.astype(F32), DT_LANE0)
    dskip_e = jnp.repeat(d_skip.astype(F32), SSM_HEADDIM).reshape(1, SSM_INNER)
    p, y_ssm, qa, ka = _inproj(
        x, row(norm_w), scale, shift, w_big, w_small, 0.5 * conv_w.astype(F32), 0.5 * row(conv_b),
        bias_row, alog_row, dskip_e, row(ssm_norm_w), tril, ee, eqk)

    y_att = _attn(p, qa, ka)

    return _outproj(y_ssm, y_att, p, x, gate, row(b_gate),
                    w_proj_ssm.astype(BF16), w_proj_att.astype(BF16), w_out.astype(BF16),
                    row(final_norm_w))
```

```python
import functools

import jax
import jax.numpy as jnp
import numpy as np
from jax import lax
from jax.experimental import pallas as pl
from jax.experimental.pallas import tpu as pltpu

F32 = jnp.float32
BF16 = jnp.bfloat16

D_MODEL = 1024
SSM_HEADDIM = 64
SSM_HEADS = 16
SSM_INNER = SSM_HEADS * SSM_HEADDIM
SSM_GROUPS = 2
SSM_STATE = 64
CONV_K = 4
CHUNK = 128
ATT_HEADS = 8
ATT_HEADDIM = 128
ATT_INNER = ATT_HEADS * ATT_HEADDIM
EPS = 1e-6
LOG2E = 1.4426950408889634

LANES = 128
GROUP_W = SSM_INNER // SSM_GROUPS
BC_W = 2 * SSM_GROUPS * SSM_STATE

COL_Z_SSM = 0
COL_XS = 1024
COL_Q = 2048
COL_K = 3072
COL_V = 4096
COL_Z_ATT = 5120
COL_G = 6144
COL_BC = 8192
P_COLS = 8448
P_OUT0 = COL_Q
P_OUT_COLS = COL_BC - COL_Q

DT_LANE0 = 0
F_LANE0 = SSM_HEADS
ONE_LANE = LANES - 1
AUG_W = LANES // ATT_HEADS

V7X_VMEM_BYTES = 64 * 1024 * 1024
VMEM_LIMIT = V7X_VMEM_BYTES * 7 // 8


def _dot(a, b):
    return jnp.dot(a, b, preferred_element_type=F32)


def _split2(x):
    hi = x.astype(BF16)
    lo = (x - hi.astype(F32)).astype(BF16)
    return hi, lo


def _split3(x):
    hi = x.astype(BF16)
    r1 = x - hi.astype(F32)
    mid = r1.astype(BF16)
    lo = (r1 - mid.astype(F32)).astype(BF16)
    return hi, mid, lo


def _sigmoid(x):
    return 0.5 * jnp.tanh(0.5 * x) + 0.5


def _silu(x):
    h = 0.5 * x
    return h + h * jnp.tanh(h)


def _ada_kernel(c_ref, w_ref, b_ref, o_ref):
    c = c_ref[...]
    cs = c * jax.nn.sigmoid(c)
    w = w_ref[...]
    c_hi, c_lo = _split2(cs)
    w_hi, w_lo = _split2(w)
    o_ref[...] = _dot(c_hi, w_hi) + _dot(c_lo, w_hi) + _dot(c_hi, w_lo) + b_ref[...]


def _ada(c_pad, w_ada, b_ada):
    rows = c_pad.shape[0]
    n = w_ada.shape[1]
    tn = 1024
    return pl.pallas_call(
        _ada_kernel,
        grid=(n // tn,),
        in_specs=[
            pl.BlockSpec((rows, D_MODEL), lambda j: (0, 0)),
            pl.BlockSpec((D_MODEL, tn), lambda j: (0, j)),
            pl.BlockSpec((1, tn), lambda j: (0, j)),
        ],
        out_specs=pl.BlockSpec((rows, tn), lambda j: (0, j)),
        out_shape=jax.ShapeDtypeStruct((rows, n), F32),
        compiler_params=pltpu.CompilerParams(
            dimension_semantics=("arbitrary",), vmem_limit_bytes=VMEM_LIMIT),
        name="ada",
    )(c_pad, w_ada, b_ada)


INPROJ_TM = 512
INPROJ_CW = 256
TAIL = 8
Q_SCALE = ATT_HEADDIM ** -0.5 * LOG2E


def _inproj_kernel(x_ref, nw_ref, scale_ref, shift_ref, w_ref, ws_ref, cw_ref, cb_ref,
                   bias_ref, alog_ref, dskip_ref, ssm_nw_ref, tril_ref, ee_ref, eqk_ref,
                   p_ref, y_ref, qa_ref, ka_ref,
                   ext_ref, z_ref, xs_ref, bc_ref, sm_ref, state_ref, carry_ref):
    tm = INPROJ_TM

    @pl.when(pl.program_id(1) == 0)
    def _():
        ext_ref[:, 0:TAIL, :] = jnp.zeros((ext_ref.shape[0], TAIL, LANES), F32)
        state_ref[...] = jnp.zeros(state_ref.shape, F32)
        carry_ref[...] = jnp.zeros(carry_ref.shape, F32)

    x = x_ref[0]
    ms = jnp.mean(x * x, axis=-1, keepdims=True)
    y = x * lax.rsqrt(ms + EPS) * nw_ref[...]
    h = y * (1.0 + scale_ref[0]) + shift_ref[0]
    h_hi, h_lo = _split2(h)
    hh = _dot(h_hi, ws_ref[...])
    sm_ref[0] = hh[:, 0:LANES] + hh[:, LANES:2 * LANES] + _dot(h_lo, ws_ref[:, 0:LANES])

    def conv_channel(col):
        if COL_XS <= col < COL_XS + SSM_INNER:
            return col - COL_XS
        if COL_BC <= col < COL_BC + BC_W:
            return SSM_INNER + col - COL_BC
        return None

    def is_ssd_side(col):
        return conv_channel(col) is not None or COL_Z_SSM <= col < COL_Z_SSM + SSM_INNER

    starts = range(0, P_COLS, INPROJ_CW)
    for col0 in [c for c in starts if is_ssd_side(c)] + [None] + [c for c in starts if not is_ssd_side(c)]:
        if col0 is None:
            _ssd_rows(z_ref, xs_ref, bc_ref, sm_ref, bias_ref, alog_ref, dskip_ref, ssm_nw_ref,
                      tril_ref, ee_ref, eqk_ref, y_ref, qa_ref, ka_ref,
                      state_ref, carry_ref)
            continue
        acc = _dot(h_hi, w_ref[:, col0:col0 + INPROJ_CW])
        if COL_Q <= col0 < COL_Q + ATT_INNER:
            acc = acc * Q_SCALE
        if COL_Z_SSM <= col0 < COL_Z_SSM + SSM_INNER:
            z_ref[0, :, col0 - COL_Z_SSM:col0 - COL_Z_SSM + INPROJ_CW] = acc.astype(BF16)
            continue
        if conv_channel(col0) is None:
            p_ref[0, :, col0 - P_OUT0:col0 - P_OUT0 + INPROJ_CW] = acc.astype(BF16)
            continue
        for t in range(INPROJ_CW // LANES):
            ch0 = conv_channel(col0 + t * LANES)
            e = ch0 // LANES
            chs = slice(ch0, ch0 + LANES)
            u = acc[:, t * LANES:(t + 1) * LANES]
            ext_ref[e, TAIL:TAIL + tm, :] = u
            a = cb_ref[:, chs] + cw_ref[CONV_K - 1:CONV_K, chs] * u
            for k in range(CONV_K - 1):
                off = TAIL - (CONV_K - 1) + k
                a = a + cw_ref[k:k + 1, chs] * ext_ref[e, off:off + tm, :]
            act = a + a * jnp.tanh(a)
            if ch0 < SSM_INNER:
                xs_ref[0, :, chs] = act
            else:
                bc_ref[0, :, ch0 - SSM_INNER:ch0 - SSM_INNER + LANES] = act
            ext_ref[e, 0:TAIL, :] = ext_ref[e, tm:tm + TAIL, :]


def _inproj(x, norm_w, scale, shift, w_big, w_small, conv_w, conv_b,
            bias_row, alog_row, dskip_e, ssm_nw, tril, ee, eqk):
    b, s, d = x.shape
    tm = INPROJ_TM
    conv_dim = SSM_INNER + BC_W
    const = lambda shape: pl.BlockSpec(shape, lambda bi, i: tuple(0 for _ in shape))
    resident = lambda shape: pl.BlockSpec(shape, lambda bi, i: tuple(0 for _ in shape),
                                          pipeline_mode=pl.Buffered(1))
    rows = lambda width: pl.BlockSpec((1, tm, width), lambda bi, i: (bi, i, 0))
    return pl.pallas_call(
        _inproj_kernel,
        grid=(b, s // tm),
        in_specs=[
            rows(d),
            const((1, d)),
            pl.BlockSpec((1, 1, d), lambda bi, i: (bi, 0, 0)),
            pl.BlockSpec((1, 1, d), lambda bi, i: (bi, 0, 0)),
            resident((d, P_COLS)),
            resident((d, 2 * LANES)),
            const((CONV_K, conv_dim)), const((1, conv_dim)),
            const((1, LANES)), const((1, LANES)),
            const((1, SSM_INNER)), const((1, SSM_INNER)),
            const((CHUNK, CHUNK)), const((2 * LANES, SSM_INNER)),
            const((3 * LANES, 2 * LANES)),
        ],
        out_specs=[rows(P_OUT_COLS), rows(SSM_INNER), rows(LANES), rows(LANES)],
        out_shape=[
            jax.ShapeDtypeStruct((b, s, P_OUT_COLS), BF16),
            jax.ShapeDtypeStruct((b, s, SSM_INNER), BF16),
            jax.ShapeDtypeStruct((b, s, LANES), BF16),
            jax.ShapeDtypeStruct((b, s, LANES), BF16),
        ],
        scratch_shapes=[
            pltpu.VMEM((conv_dim // LANES, tm + TAIL, LANES), F32),
            pltpu.VMEM((1, tm, SSM_INNER), BF16),
            pltpu.VMEM((1, tm, SSM_INNER), F32),
            pltpu.VMEM((1, tm, BC_W), F32),
            pltpu.VMEM((1, tm, LANES), F32),
            pltpu.VMEM((SSM_GROUPS, LANES, GROUP_W), F32),
            pltpu.VMEM((1, LANES), F32),
        ],
        compiler_params=pltpu.CompilerParams(
            dimension_semantics=("arbitrary", "arbitrary"), vmem_limit_bytes=VMEM_LIMIT),
        name="inproj_ssd",
    )(x, norm_w, scale, shift, w_big, w_small, conv_w, conv_b,
      bias_row, alog_row, dskip_e, ssm_nw, tril, ee, eqk)


def _ssd_rows(z_ref, xs_ref, bc_ref, sm_ref,
              bias_ref, alog_ref, dskip_ref, nw_ref, tril_ref, ee_ref, eqk_ref,
              y_ref, qa_ref, ka_ref,
              state_ref, carry_ref):
    rows = z_ref.shape[1]
    L = CHUNK

    lane = lax.broadcasted_iota(jnp.int32, (L, LANES), 1)
    is_dt = lane < SSM_HEADS
    is_f = jnp.logical_and(lane >= F_LANE0, lane < F_LANE0 + ATT_HEADS)
    row_i = lax.broadcasted_iota(jnp.int32, (L, L), 0)
    col_i = lax.broadcasted_iota(jnp.int32, (L, L), 1)
    causal = row_i >= col_i
    neg_a = -jnp.exp(alog_ref[...]) * LOG2E
    tril = tril_ref[...]

    def chunk(c):
        r0 = c * L
        v = sm_ref[0, pl.ds(r0, L), :] + bias_ref[...]
        t = jnp.log(1.0 + jnp.exp(-jnp.abs(v)))
        dt = jnp.maximum(v, 0.0) + t
        lf = jnp.minimum(v, 0.0) - t
        val = jnp.where(is_dt, dt * neg_a, jnp.where(is_f, lf, 0.0))
        v_hi, v_mid, v_lo = _split3(val)
        cs3 = _dot(tril, jnp.concatenate([v_hi, v_mid, v_lo], axis=1))
        cs = cs3[:, 0:LANES] + cs3[:, LANES:2 * LANES] + cs3[:, 2 * LANES:3 * LANES]

        fc = jnp.where(is_f, cs + carry_ref[...], 0.0)
        carry_ref[...] = fc[L - 1:L, :]
        fa = jnp.where(lane == ONE_LANE, 1.0, fc * LOG2E)
        f_hi, f_mid, f_lo = _split3(fa)
        f3 = jnp.concatenate([f_hi, f_mid, f_lo], axis=1)
        qk = _dot(f3, eqk_ref[...]).astype(BF16)
        qa_ref[0, pl.ds(r0, L), :] = qk[:, 0:LANES]
        ka_ref[0, pl.ds(r0, L), :] = qk[:, LANES:2 * LANES]

        a_cs = jnp.where(is_dt, cs, 0.0)
        a_last = a_cs[L - 1:L, :]
        dtm = jnp.where(is_dt, dt, 0.0)
        w2 = jnp.where(is_dt, jnp.exp2(a_cs), 0.0)
        w1 = dtm * jnp.exp2(a_last - a_cs)
        st_hi, st_lo = _split2(jnp.concatenate([dtm, w2, w1], axis=0))
        ex = _dot(jnp.concatenate([st_hi, st_lo], axis=1), ee_ref[...])
        dt_e = ex[0:L]
        w2_e = ex[L:2 * L]
        w1_e = ex[2 * L:3 * L]

        xsc = xs_ref[0, pl.ds(r0, L), :]
        bcv = bc_ref[0, pl.ds(r0, L), :]
        b_all = bcv[:, 0:LANES]
        c_all = bcv[:, LANES:2 * LANES]
        b_t = b_all.T.astype(BF16)
        xd = xsc * dt_e
        xw1 = (xsc * w1_e).astype(BF16)
        a_cs_t = a_cs.T

        ys = []
        for g in range(SSM_GROUPS):
            gsl = slice(g * GROUP_W, (g + 1) * GROUP_W)
            in_g = jnp.logical_and(lane >= g * SSM_STATE, lane < (g + 1) * SSM_STATE)
            c_m = jnp.where(in_g, c_all, 0.0).astype(BF16)
            cb = jnp.where(causal, _dot(c_m, b_t), 0.0).astype(BF16)
            s_g = state_ref[g]
            y_off = _dot(c_m, s_g.astype(BF16)) * w2_e[:, gsl]
            pieces = []
            for j in range(GROUP_W // LANES):
                h_a = g * (SSM_HEADS // SSM_GROUPS) + 2 * j
                xp = xd[:, g * GROUP_W + j * LANES:g * GROUP_W + (j + 1) * LANES]
                x_a = jnp.where(lane < SSM_HEADDIM, xp, 0.0).astype(BF16)
                x_b = jnp.where(lane >= SSM_HEADDIM, xp, 0.0).astype(BF16)

                def lmat(h):
                    seg = (a_cs[:, h:h + 1] - a_cs_t[h:h + 1, :]).astype(BF16)
                    return cb * jnp.exp2(jnp.minimum(seg, 0))

                pieces.append(_dot(jnp.concatenate([lmat(h_a), lmat(h_a + 1)], axis=1),
                                   jnp.concatenate([x_a, x_b], axis=0)))
            ys.append(jnp.concatenate(pieces, axis=1) + y_off)
            state_ref[g] = s_g * w2_e[L - 1:L, gsl] + _dot(b_t, xw1[:, gsl])

        y = jnp.concatenate(ys, axis=1) + dskip_ref[...] * xsc
        z = z_ref[0, pl.ds(r0, L), :].astype(F32)
        yg = y * _silu(z)
        ms = jnp.mean(yg * yg, axis=-1, keepdims=True)
        y_ref[0, pl.ds(r0, L), :] = (yg * lax.rsqrt(ms + EPS) * nw_ref[...]).astype(BF16)

    for c in range(rows // L):
        chunk(c)


ATT_TQ = 512
ATT_TK = ATT_TQ
ATT_SPARE_SLOT = 2
ATT_HPS = 2
ATT_QSUB = 8
ATT_UNROLL = 4


def _attn_kernel(q_ref, qa_ref, k_ref, ka_ref, v_ref, z_ref, o_ref,
                 kam_ref, s_ref, mx_ref, m_ref, acc_ref):
    tk = ATT_TK

    @pl.when(pl.program_id(2) == 0)
    def _():
        lane_head = lax.broadcasted_iota(jnp.int32, (tk, LANES), 1) // AUG_W
        for hh in range(ATT_HPS):
            own = lane_head == pl.program_id(1) * ATT_HPS + hh

            def fill(t, carry):
                r0 = pl.multiple_of(t * tk, tk)
                blk = ka_ref[0, pl.ds(r0, tk), :].astype(F32)
                kam_ref[hh, pl.ds(r0, tk), :] = jnp.where(own, blk, 0.0).astype(BF16)
                return carry

            lax.fori_loop(0, ka_ref.shape[1] // tk, fill, 0)

    refs = (q_ref, qa_ref, k_ref, kam_ref, v_ref, z_ref, o_ref, s_ref, mx_ref, m_ref, acc_ref)
    _attn_first_scores(_attn_q_operand(0, q_ref, qa_ref), k_ref, kam_ref, s_ref, mx_ref)

    def sub_tile(sub, carry):
        _attn_q_tile(sub, *refs)
        return carry

    lax.fori_loop(0, ATT_QSUB, sub_tile, 0)


def _attn_q_operand(sub, q_ref, qa_ref):
    hd = ATT_HEADDIM
    rows = pl.ds(pl.multiple_of(sub * ATT_TQ, ATT_TQ), ATT_TQ)
    qa = qa_ref[0, rows, :]
    return [jnp.concatenate([q_ref[0, rows, hh * hd:(hh + 1) * hd], qa], axis=1)
            for hh in range(ATT_HPS)]


def _attn_first_scores(qq, k_ref, kam_ref, s_ref, mx_ref):
    hd = ATT_HEADDIM
    for hh in range(ATT_HPS):
        kk = jnp.concatenate([k_ref[0, 0:ATT_TK, hh * hd:(hh + 1) * hd], kam_ref[hh, 0:ATT_TK, :]],
                             axis=1)
        s = lax.dot_general(qq[hh], kk, (((1,), (1,)), ((), ())), preferred_element_type=F32)
        s_ref[0, hh] = s
        mx_ref[0, hh] = jnp.broadcast_to(jnp.max(s, axis=1, keepdims=True), (ATT_TQ, LANES))


def _attn_q_tile(sub, q_ref, qa_ref, k_ref, kam_ref, v_ref, z_ref, o_ref,
                 s_ref, mx_ref, m_ref, acc_ref):
    tq, tk, hd = ATT_TQ, ATT_TK, ATT_HEADDIM
    heads = range(ATT_HPS)
    hsl = lambda hh: slice(hh * hd, (hh + 1) * hd)
    qi = pl.program_id(2) * ATT_QSUB + sub
    qrows = pl.ds(pl.multiple_of(sub * tq, tq), tq)

    qq = _attn_q_operand(sub, q_ref, qa_ref)
    m_ref[...] = jnp.full(m_ref.shape, -jnp.inf, F32)
    acc_ref[...] = jnp.zeros(acc_ref.shape, F32)
    ones_v = jnp.ones((tk, hd), BF16)

    def scores(j, slot):
        r0 = pl.multiple_of(j * tk, tk)
        for hh in heads:
            kk = jnp.concatenate(
                [k_ref[0, pl.ds(r0, tk), hsl(hh)], kam_ref[hh, pl.ds(r0, tk), :]], axis=1)
            s = lax.dot_general(qq[hh], kk, (((1,), (1,)), ((), ())), preferred_element_type=F32)
            s_ref[slot, hh] = s
            mx_ref[slot, hh] = jnp.broadcast_to(jnp.max(s, axis=1, keepdims=True), (tq, LANES))

    def update(j, slot, masked=False):
        r0 = pl.multiple_of(j * tk, tk)
        for hh in heads:
            s = s_ref[slot, hh]
            if masked:
                row = lax.broadcasted_iota(jnp.int32, (tq, tk), 0)
                col = lax.broadcasted_iota(jnp.int32, (tq, tk), 1)
                s = jnp.where(row >= col, s, -jnp.inf)
                mx = jnp.broadcast_to(jnp.max(s, axis=1, keepdims=True), (tq, LANES))
            else:
                mx = mx_ref[slot, hh]
            m_old = m_ref[hh]
            m_new = jnp.maximum(m_old, mx)
            alpha = jnp.exp2(m_old - m_new)
            p = jnp.concatenate(
                [jnp.exp2((s[:, c * LANES:(c + 1) * LANES] - m_new).astype(BF16))
                 for c in range(tk // LANES)], axis=1)
            vv = jnp.concatenate([v_ref[0, pl.ds(r0, tk), hsl(hh)], ones_v], axis=1)
            pv = _dot(p, vv)
            acc_ref[hh, :, 0:hd] = alpha * acc_ref[hh, :, 0:hd] + pv[:, 0:hd]
            acc_ref[hh, :, hd:2 * hd] = alpha * acc_ref[hh, :, hd:2 * hd] + pv[:, hd:2 * hd]
            m_ref[hh] = m_new

    def run(j0, n, last_slot=None):
        for u in range(n):
            last = u == n - 1 and last_slot is not None
            scores(j0 + u + 1, last_slot if last else (u + 1) % 2)
            update(j0 + u, u % 2)

    def body(i, carry):
        run(ATT_UNROLL * i, ATT_UNROLL)
        return carry

    def next_first_scores():
        nxt = jnp.minimum(sub + 1, ATT_QSUB - 1)
        _attn_first_scores(_attn_q_operand(nxt, q_ref, qa_ref), k_ref, kam_ref, s_ref, mx_ref)

    lax.fori_loop(0, qi // ATT_UNROLL, body, 0)
    for r in range(ATT_UNROLL):
        @pl.when(qi % ATT_UNROLL == r)
        def _(r=r):
            if r == 0:
                update(qi, 0, masked=True)
                next_first_scores()
            else:
                run(qi - r, r, last_slot=ATT_SPARE_SLOT)
                next_first_scores()
                update(qi, ATT_SPARE_SLOT, masked=True)

    for hh in heads:
        z = z_ref[0, qrows, hsl(hh)].astype(F32)
        o = acc_ref[hh, :, 0:hd] / acc_ref[hh, :, hd:2 * hd]
        o_ref[0, qrows, hsl(hh)] = (o * _silu(z)).astype(BF16)


def _attn(p, qa, ka):
    b, s, _ = p.shape
    tq = ATT_TQ * ATT_QSUB
    w = ATT_HPS * ATT_HEADDIM
    return pl.pallas_call(
        _attn_kernel,
        grid=(b, ATT_HEADS // ATT_HPS, s // tq),
        in_specs=[
            pl.BlockSpec((1, tq, w), lambda bi, h, i: (bi, i, (COL_Q - P_OUT0) // w + h)),
            pl.BlockSpec((1, tq, LANES), lambda bi, h, i: (bi, i, 0)),
            pl.BlockSpec((1, s, w), lambda bi, h, i: (bi, 0, (COL_K - P_OUT0) // w + h)),
            pl.BlockSpec((1, s, LANES), lambda bi, h, i: (bi, 0, 0)),
            pl.BlockSpec((1, s, w), lambda bi, h, i: (bi, 0, (COL_V - P_OUT0) // w + h)),
            pl.BlockSpec((1, tq, w), lambda bi, h, i: (bi, i, (COL_Z_ATT - P_OUT0) // w + h)),
        ],
        out_specs=pl.BlockSpec((1, tq, w), lambda bi, h, i: (bi, i, h)),
        out_shape=jax.ShapeDtypeStruct((b, s, ATT_INNER), BF16),
        scratch_shapes=[
            pltpu.VMEM((ATT_HPS, s, LANES), BF16),
            pltpu.VMEM((3, ATT_HPS, ATT_TQ, ATT_TK), F32),
            pltpu.VMEM((3, ATT_HPS, ATT_TQ, LANES), F32),
            pltpu.VMEM((ATT_HPS, ATT_TQ, LANES), F32),
            pltpu.VMEM((ATT_HPS, ATT_TQ, 2 * ATT_HEADDIM), F32),
        ],
        compiler_params=pltpu.CompilerParams(
            dimension_semantics=("arbitrary", "arbitrary", "arbitrary"),
            vmem_limit_bytes=VMEM_LIMIT),
        name="fox_attn",
    )(p, qa, p, ka, p, p)


OUT_TM = 1024


def _out_kernel(ys_ref, ya_ref, g_ref, x_ref, gate_ref, bg_ref, wps_ref, wpa_ref, wo_ref,
                fnw_ref, o_ref):
    g = _sigmoid(g_ref[0].astype(F32) + bg_ref[...])
    ps = _dot(ys_ref[0], wps_ref[...])
    pa = _dot(ya_ref[0], wpa_ref[...])
    merged = g[:, 0:D_MODEL] * ps + g[:, D_MODEL:2 * D_MODEL] * pa
    out = _dot(merged.astype(BF16), wo_ref[...])
    xn = x_ref[0] + gate_ref[0] * out
    ms = jnp.mean(xn * xn, axis=-1, keepdims=True)
    o_ref[0] = xn * lax.rsqrt(ms + EPS) * fnw_ref[...]


def _outproj(y_ssm, y_att, p, x, gate, b_gate, wps, wpa, wo, fnw):
    b, s, d = x.shape
    tm = OUT_TM
    const = lambda shape: pl.BlockSpec(shape, lambda bi, i: tuple(0 for _ in shape))
    return pl.pallas_call(
        _out_kernel,
        grid=(b, s // tm),
        in_specs=[
            pl.BlockSpec((1, tm, d), lambda bi, i: (bi, i, 0)),
            pl.BlockSpec((1, tm, d), lambda bi, i: (bi, i, 0)),
            pl.BlockSpec((1, tm, 2 * d), lambda bi, i: (bi, i, (COL_G - P_OUT0) // (2 * d))),
            pl.BlockSpec((1, tm, d), lambda bi, i: (bi, i, 0)),
            pl.BlockSpec((1, 1, d), lambda bi, i: (bi, 0, 0)),
            const((1, 2 * d)), const((d, d)), const((d, d)), const((d, d)), const((1, d)),
        ],
        out_specs=pl.BlockSpec((1, tm, d), lambda bi, i: (bi, i, 0)),
        out_shape=jax.ShapeDtypeStruct((b, s, d), F32),
        compiler_params=pltpu.CompilerParams(
            dimension_semantics=("arbitrary", "arbitrary"), vmem_limit_bytes=VMEM_LIMIT),
        name="outproj",
    )(y_ssm, y_att, p, x, gate, b_gate, wps, wpa, wo, fnw)


def _constants():
    tril = np.tril(np.ones((CHUNK, CHUNK), np.float32))
    ee = np.zeros((2 * LANES, SSM_INNER), np.float32)
    for h in range(SSM_HEADS):
        ee[h, h * SSM_HEADDIM:(h + 1) * SSM_HEADDIM] = 1.0
        ee[LANES + h, h * SSM_HEADDIM:(h + 1) * SSM_HEADDIM] = 1.0
    eq = np.zeros((3 * LANES, LANES), np.float32)
    ek = np.zeros((3 * LANES, LANES), np.float32)
    for h in range(ATT_HEADS):
        base = h * AUG_W
        for part in range(3):
            eq[part * LANES + F_LANE0 + h, base + 3 + part] = 1.0
            ek[part * LANES + F_LANE0 + h, base + part] = -1.0
            eq[ONE_LANE, base + part] = 1.0
            ek[ONE_LANE, base + 3 + part] = 1.0
    as_bf16 = lambda a: jnp.asarray(a, dtype=BF16)
    return as_bf16(tril), as_bf16(ee), as_bf16(np.concatenate([eq, ek], axis=1))


def kernel(x, c, w_ada, b_ada, norm_w, w_in, conv_w, conv_b, dt_bias, a_log, d_skip,
           ssm_norm_w, b_f, b_gate, w_proj_ssm, w_proj_att, w_out, final_norm_w):
    b, s, d = x.shape
    row = lambda v: v.reshape(1, -1).astype(F32)

    c_pad = jnp.zeros((8, d), F32).at[0:b].set(c)
    ada = _ada(c_pad, w_ada, row(b_ada))[0:b]
    shift = ada[:, 0:d].reshape(b, 1, d)
    scale = ada[:, d:2 * d].reshape(b, 1, d)
    gate = ada[:, 2 * d:3 * d].reshape(b, 1, d)

    o_z, o_xbc, o_dt, o_q, o_k, o_v, o_za, o_f, o_g = np.cumsum(
        [0, SSM_INNER, SSM_INNER + BC_W, SSM_HEADS, ATT_INNER, ATT_INNER, ATT_INNER,
         ATT_INNER, ATT_HEADS]).tolist()
    w_bf = w_in.astype(BF16)
    w_big = jnp.concatenate([
        w_bf[:, o_z:o_z + SSM_INNER],
        w_bf[:, o_xbc:o_xbc + SSM_INNER],
        w_bf[:, o_q:o_q + ATT_INNER],
        w_bf[:, o_k:o_k + ATT_INNER],
        w_bf[:, o_v:o_v + ATT_INNER],
        w_bf[:, o_za:o_za + ATT_INNER],
        w_bf[:, o_g:o_g + 2 * d],
        w_bf[:, o_xbc + SSM_INNER:o_xbc + SSM_INNER + BC_W],
    ], axis=1)
    w_sm = jnp.concatenate([
        w_in[:, o_dt:o_dt + SSM_HEADS], w_in[:, o_f:o_f + ATT_HEADS],
        jnp.zeros((d, LANES - SSM_HEADS - ATT_HEADS), F32)], axis=1)
    w_sm_hi = w_sm.astype(BF16)
    w_sm_lo = (w_sm - w_sm_hi.astype(F32)).astype(BF16)
    w_small = jnp.concatenate([w_sm_hi, w_sm_lo], axis=1)

    tril, ee, eqk = _constants()
    pad_lanes = lambda v, lane0: jnp.zeros((1, LANES), F32).at[0, lane0:lane0 + v.shape[0]].set(v)
    bias_row = pad_lanes(dt_bias.astype(F32), DT_LANE0) + pad_lanes(b_f.astype(F32), F_LANE0)
    alog_row = pad_lanes(a_log.astype(F32), DT_LANE0)
    dskip_e = jnp.repeat(d_skip.astype(F32), SSM_HEADDIM).reshape(1, SSM_INNER)
    p, y_ssm, qa, ka = _inproj(
        x, row(norm_w), scale, shift, w_big, w_small, 0.5 * conv_w.astype(F32), 0.5 * row(conv_b),
        bias_row, alog_row, dskip_e, row(ssm_norm_w), tril, ee, eqk)

    y_att = _attn(p, qa, ka)

    return _outproj(y_ssm, y_att, p, x, gate, row(b_gate),
                    w_proj_ssm.astype(BF16), w_proj_att.astype(BF16), w_out.astype(BF16),
                    row(final_norm_w))
```
